```python
import functools
import jax, jax.numpy as jnp
from jax import lax
import numpy as np

D_MODEL = 2048
BATCH = 4
SEQ = 4096
DEPTH = 1
DEC_BATCH = 16
DEC_SEQ = 64
PAST_LEN = 2048

CHUNK = 64
CONV_CH = D_MODEL // 2
CONV_W = 3
RWKV_DIM = D_MODEL - CONV_CH
HEAD_DIM = 64
N_HEADS = RWKV_DIM // HEAD_DIM
DECAY_LORA = 64
AAA_LORA = 64
GATE_LORA = 160
D_FF = ((8 * D_MODEL // 3) + 127) // 128 * 128
MIX_DIM = CONV_CH + RWKV_DIM
SHIFT_DIM = 3 * RWKV_DIM + DECAY_LORA + AAA_LORA + GATE_LORA
PROJ_DIM = 3 * CONV_CH + SHIFT_DIM
RMS_EPS = 1e-6
GN_EPS = 64e-5
L2_EPS = 1e-12

kernel_name = "hybrid_shortconv_rwkv7_macaron_step"


def rmsnorm(x, g):
    xf = x.astype(jnp.float32)
    y = xf * lax.rsqrt(jnp.mean(xf * xf, axis=-1, keepdims=True) + RMS_EPS)
    return (y * g.astype(jnp.float32)).astype(x.dtype)


def swiglu(x, wg, wu, wd):
    return (jax.nn.silu(x @ wg) * (x @ wu)) @ wd


def wkv_scan(S0, r, w, k, v, kk, a):
    Bsz, T, H, N = r.shape
    blk = CHUNK if T % CHUNK == 0 else T
    nb = T // blk

    def tm(z):
        return jnp.moveaxis(z, 1, 0).reshape(nb, blk, Bsz, H, N)

    seqs = (tm(r), tm(w), tm(k), tm(v), tm(kk), tm(a))

    def step(S, inp):
        r_t, w_t, k_t, v_t, kk_t, a_t = inp
        s_kk = jnp.einsum('bhvk,bhk->bhv', S, kk_t)
        S = (S * w_t[:, :, None, :]
             - s_kk[..., None] * (kk_t * a_t)[:, :, None, :]
             + v_t[..., None] * k_t[:, :, None, :])
        return S, jnp.einsum('bhvk,bhk->bhv', S, r_t)

    def block(S, blk_inp):
        return lax.scan(step, S, blk_inp)

    S, ys = lax.scan(block, S0, seqs)
    y = jnp.moveaxis(ys.reshape(T, Bsz, H, N), 0, 1)
    return y, S


def token_mixers(z, conv_prev, shift_prev, wkv_prev, w_in, conv_w, tshift_mu, w0,
                 w_decay_up, a0, w_a_up, w_g_up, k_k, k_a, r_k, ln_x_w, ln_x_b, w_out):
    Bsz, T, _ = z.shape
    f32 = jnp.float32
    p = z @ w_in
    gate_b, gate_c, h_conv, p_rw = jnp.split(p, [CONV_CH, 2 * CONV_CH, 3 * CONV_CH], axis=-1)

    u = gate_c * h_conv
    u_ext = jnp.concatenate([conv_prev.astype(u.dtype), u], axis=1)
    conv = u_ext[:, 0:T] * conv_w[0]
    for j in range(1, CONV_W):
        conv = conv + u_ext[:, j:j + T] * conv_w[j]
    y_conv = gate_b * conv
    new_conv = u_ext[:, T:]

    p_ext = jnp.concatenate([shift_prev.astype(p_rw.dtype), p_rw], axis=1)
    p_prev = p_ext[:, :T]
    new_shift = p_ext[:, T:]
    q = (p_rw + (p_prev - p_rw) * tshift_mu).astype(f32)
    R = RWKV_DIM
    r, k, v, xw, xa, xg = jnp.split(
        q, [R, 2 * R, 3 * R, 3 * R + DECAY_LORA, 3 * R + DECAY_LORA + AAA_LORA], axis=-1)
    w_log = -jax.nn.softplus(-(w0.astype(f32) + jnp.tanh(xw) @ w_decay_up.astype(f32))) - 0.5
    decay = jnp.exp(-jnp.exp(w_log))
    a = jax.nn.sigmoid(a0.astype(f32) + xa @ w_a_up.astype(f32))
    g = jax.nn.sigmoid(xg) @ w_g_up.astype(f32)

    def heads(t):
        return t.reshape(Bsz, T, N_HEADS, HEAD_DIM)

    kk = heads(k * k_k.astype(f32))
    kk = kk / jnp.maximum(jnp.sqrt(jnp.sum(kk * kk, axis=-1, keepdims=True)), L2_EPS)
    k = k * (1.0 + (a - 1.0) * k_a.astype(f32))
    rh, kh, vh = heads(r), heads(k), heads(v)
    y, S = wkv_scan(wkv_prev.astype(f32), rh, heads(decay), kh, vh, kk, heads(a))
    mu = jnp.mean(y, axis=-1, keepdims=True)
    var = jnp.mean(jnp.square(y - mu), axis=-1, keepdims=True)
    y = ((y - mu) * lax.rsqrt(var + GN_EPS)).reshape(Bsz, T, R)
    y = y * ln_x_w.astype(f32) + ln_x_b.astype(f32)
    bonus = jnp.sum(rh * kh * r_k.astype(f32), axis=-1, keepdims=True) * vh
    y_rwkv = ((y + bonus.reshape(Bsz, T, R)) * g).astype(z.dtype)

    mix = jnp.concatenate([y_conv, y_rwkv], axis=-1)
    return mix @ w_out, new_conv, new_shift, S


def layer(x, conv_prev, shift_prev, wkv_prev, g_ffn1_pre, w_ffn1_gate, w_ffn1_up,
          w_ffn1_down, g_ffn1_post, g_mix_pre, w_in, conv_w, tshift_mu, w0, w_decay_up,
          a0, w_a_up, w_g_up, k_k, k_a, r_k, ln_x_w, ln_x_b, w_out, g_mix_post,
          g_ffn2_pre, w_ffn2_gate, w_ffn2_up, w_ffn2_down, g_ffn2_post):
    h = x + 0.5 * rmsnorm(swiglu(rmsnorm(x, g_ffn1_pre), w_ffn1_gate, w_ffn1_up, w_ffn1_down), g_ffn1_post)
    m, new_conv, new_shift, new_wkv = token_mixers(
        rmsnorm(h, g_mix_pre), conv_prev, shift_prev, wkv_prev, w_in, conv_w, tshift_mu, w0,
        w_decay_up, a0, w_a_up, w_g_up, k_k, k_a, r_k, ln_x_w, ln_x_b, w_out)
    h = h + rmsnorm(m, g_mix_post)
    h = h + 0.5 * rmsnorm(swiglu(rmsnorm(h, g_ffn2_pre), w_ffn2_gate, w_ffn2_up, w_ffn2_down), g_ffn2_post)
    return h, new_conv, new_shift, new_wkv


def setup_inputs(seed: int = 0) -> dict:
    key = jax.random.key(seed)
    ks = iter(jax.random.split(key, 40))

    def nrm(shape, scale):
        return scale * jax.random.normal(next(ks), shape, jnp.float32)

    def gain(n):
        return 1.0 + nrm((DEPTH, n), 0.05)

    L = DEPTH
    return {
        "x_prompt": nrm((BATCH, SEQ, D_MODEL), 1.0),
        "x_sample": nrm((DEC_BATCH, DEC_SEQ, D_MODEL), 1.0),
        "cache_conv": nrm((L, DEC_BATCH, CONV_W - 1, CONV_CH), 1.0),
        "cache_shift": nrm((L, DEC_BATCH, 1, SHIFT_DIM), 1.0),
        "state_wkv": nrm((L, DEC_BATCH, N_HEADS, HEAD_DIM, HEAD_DIM), 0.3),
        "g_ffn1_pre": gain(D_MODEL),
        "w_ffn1_gate": nrm((L, D_MODEL, D_FF), D_MODEL ** -0.5),
        "w_ffn1_up": nrm((L, D_MODEL, D_FF), D_MODEL ** -0.5),
        "w_ffn1_down": nrm((L, D_FF, D_MODEL), D_FF ** -0.5),
        "g_ffn1_post": gain(D_MODEL),
        "g_mix_pre": gain(D_MODEL),
        "w_in": nrm((L, D_MODEL, PROJ_DIM), D_MODEL ** -0.5),
        "conv_w": nrm((L, CONV_W, CONV_CH), 0.5),
        "tshift_mu": jax.random.uniform(next(ks), (L, SHIFT_DIM), jnp.float32),
        "w0": nrm((L, RWKV_DIM), 0.5),
        "w_decay_up": nrm((L, DECAY_LORA, RWKV_DIM), 0.1),
        "a0": nrm((L, RWKV_DIM), 0.1),
        "w_a_up": nrm((L, AAA_LORA, RWKV_DIM), 0.1),
        "w_g_up": nrm((L, GATE_LORA, RWKV_DIM), GATE_LORA ** -0.5),
        "k_k": 0.85 + nrm((L, RWKV_DIM), 0.05),
        "k_a": 1.0 + nrm((L, RWKV_DIM), 0.05),
        "r_k": nrm((L, N_HEADS, HEAD_DIM), 0.1),
        "ln_x_w": gain(RWKV_DIM),
        "ln_x_b": nrm((L, RWKV_DIM), 0.02),
        "w_out": nrm((L, MIX_DIM, D_MODEL), MIX_DIM ** -0.5),
        "g_mix_post": gain(D_MODEL),
        "g_ffn2_pre": gain(D_MODEL),
        "w_ffn2_gate": nrm((L, D_MODEL, D_FF), D_MODEL ** -0.5),
        "w_ffn2_up": nrm((L, D_MODEL, D_FF), D_MODEL ** -0.5),
        "w_ffn2_down": nrm((L, D_FF, D_MODEL), D_FF ** -0.5),
        "g_ffn2_post": gain(D_MODEL),
    }


def reference(x_prompt, x_sample, cache_conv, cache_shift, state_wkv, g_ffn1_pre,
              w_ffn1_gate, w_ffn1_up, w_ffn1_down, g_ffn1_post, g_mix_pre, w_in, conv_w,
              tshift_mu, w0, w_decay_up, a0, w_a_up, w_g_up, k_k, k_a, r_k, ln_x_w,
              ln_x_b, w_out, g_mix_post, g_ffn2_pre, w_ffn2_gate, w_ffn2_up, w_ffn2_down,
              g_ffn2_post):
    bp = x_prompt.shape[0]
    yp, ys = x_prompt, x_sample
    conv_p, shift_p, wkv_p, conv_s, shift_s, wkv_s = [], [], [], [], [], []
    for l in range(DEPTH):
        run = functools.partial(
            layer, g_ffn1_pre=g_ffn1_pre[l], w_ffn1_gate=w_ffn1_gate[l], w_ffn1_up=w_ffn1_up[l],
            w_ffn1_down=w_ffn1_down[l], g_ffn1_post=g_ffn1_post[l], g_mix_pre=g_mix_pre[l],
            w_in=w_in[l], conv_w=conv_w[l], tshift_mu=tshift_mu[l], w0=w0[l],
            w_decay_up=w_decay_up[l], a0=a0[l], w_a_up=w_a_up[l], w_g_up=w_g_up[l],
            k_k=k_k[l], k_a=k_a[l], r_k=r_k[l], ln_x_w=ln_x_w[l], ln_x_b=ln_x_b[l],
            w_out=w_out[l], g_mix_post=g_mix_post[l], g_ffn2_pre=g_ffn2_pre[l],
            w_ffn2_gate=w_ffn2_gate[l], w_ffn2_up=w_ffn2_up[l], w_ffn2_down=w_ffn2_down[l],
            g_ffn2_post=g_ffn2_post[l])
        zc = jnp.zeros((bp, CONV_W - 1, CONV_CH), cache_conv.dtype)
        zs = jnp.zeros((bp, 1, SHIFT_DIM), cache_shift.dtype)
        zw = jnp.zeros((bp, N_HEADS, HEAD_DIM, HEAD_DIM), jnp.float32)
        yp, c1, s1, w1 = run(yp, zc, zs, zw)
        ys, c2, s2, w2 = run(ys, cache_conv[l], cache_shift[l], state_wkv[l])
        conv_p.append(c1.astype(cache_conv.dtype))
        shift_p.append(s1.astype(cache_shift.dtype))
        wkv_p.append(w1.astype(state_wkv.dtype))
        conv_s.append(c2.astype(cache_conv.dtype))
        shift_s.append(s2.astype(cache_shift.dtype))
        wkv_s.append(w2.astype(state_wkv.dtype))
    new_conv_p = jnp.stack(conv_p, 0)
    new_shift_p = jnp.stack(shift_p, 0)
    new_wkv_p = jnp.stack(wkv_p, 0)
    new_conv_s = jnp.stack(conv_s, 0)
    new_shift_s = jnp.stack(shift_s, 0)
    new_wkv_s = jnp.stack(wkv_s, 0)
    return (yp, ys, new_conv_p, new_shift_p, new_wkv_p, new_conv_s, new_shift_s, new_wkv_s)
```

```python
import functools

import jax
import jax.numpy as jnp
from jax import lax
from jax.experimental import pallas as pl
from jax.experimental.pallas import tpu as pltpu

F32 = jnp.float32
BF16 = jnp.bfloat16

RMS_EPS = 1e-6
GN_EPS = 64e-5
L2_EPS = 1e-12
HEAD_DIM = 64
WKV_CHUNK = 64
LANES = 128
LORA_BLOCK = 512
VMEM_LIMIT = 56 * 1024 * 1024
TILES = dict(ffn_tm=512, ffn_tf=512, inproj_tm=512, inproj_tn=512, prep_tt=256, wkv_tb=512, wkv_hpb=2, mix_tm=256)


def _cparams(sem):
    return pltpu.CompilerParams(dimension_semantics=sem, vmem_limit_bytes=VMEM_LIMIT)


def _round_up(n, m):
    return (n + m - 1) // m * m


def _row_tile(m, pref):
    return pref if m % pref == 0 else m


def _rms_scale(x):
    return lax.rsqrt(jnp.mean(x * x, axis=-1, keepdims=True) + RMS_EPS)


def _dot(a, b):
    return jnp.dot(a, b, preferred_element_type=F32)


def _dot_nt(a, b):
    return lax.dot_general(a, b, (((1,), (1,)), ((), ())), preferred_element_type=F32)


def _dot_tn(a, b):
    return lax.dot_general(a, b, (((0,), (0,)), ((), ())), preferred_element_type=F32)


def _split2(x):
    hi = x.astype(BF16)
    lo = (x - hi.astype(F32)).astype(BF16)
    return hi, lo


def _mm3(a, b, dot=_dot):
    ah, al = _split2(a)
    bh, bl = _split2(b)
    return dot(ah, bh) + (dot(ah, bl) + dot(al, bh))


def _head_sum(x, bd):
    hi, lo = _split2(x)
    return _dot(hi, bd) + _dot(lo, bd)


def _ffn_body(x_ref, gpre_ref, wg_ref, wu_ref, wd_ref, gpost_ref, o_ref, xn_ref, acc_ref):
    j = pl.program_id(1)

    @pl.when(j == 0)
    def _init():
        x = x_ref[...]
        xn_ref[...] = (x * _rms_scale(x) * gpre_ref[...]).astype(BF16)
        acc_ref[...] = jnp.zeros_like(acc_ref)

    xn = xn_ref[...]
    gate = _dot(xn, wg_ref[...])
    up = _dot(xn, wu_ref[...])
    hmid = (gate * jax.nn.sigmoid(gate) * up).astype(BF16)
    acc_ref[...] += _dot(hmid, wd_ref[...])

    @pl.when(j == pl.num_programs(1) - 1)
    def _fin():
        a = acc_ref[...]
        o_ref[...] = x_ref[...] + 0.5 * (a * _rms_scale(a) * gpost_ref[...])


def _ffn(x, g_pre, wg, wu, wd, g_post):
    m, d = x.shape
    f = wg.shape[1]
    tm = _row_tile(m, TILES["ffn_tm"])
    tf = _row_tile(f, TILES["ffn_tf"])
    return pl.pallas_call(
        _ffn_body,
        out_shape=jax.ShapeDtypeStruct((m, d), F32),
        grid=(m // tm, f // tf),
        in_specs=[
            pl.BlockSpec((tm, d), lambda i, j: (i, 0)),
            pl.BlockSpec((1, d), lambda i, j: (0, 0)),
            pl.BlockSpec((d, tf), lambda i, j: (0, j)),
            pl.BlockSpec((d, tf), lambda i, j: (0, j)),
            pl.BlockSpec((tf, d), lambda i, j: (j, 0)),
            pl.BlockSpec((1, d), lambda i, j: (0, 0)),
        ],
        out_specs=pl.BlockSpec((tm, d), lambda i, j: (i, 0)),
        scratch_shapes=[pltpu.VMEM((tm, d), BF16), pltpu.VMEM((tm, d), F32)],
        compiler_params=_cparams(("parallel", "arbitrary")),
        name="ffn",
    )(x, g_pre, wg, wu, wd, g_post)


def _inproj_body(x_ref, g_ref, w_ref, o_ref, xn_ref):
    @pl.when(pl.program_id(1) == 0)
    def _init():
        x = x_ref[...]
        xn_ref[...] = (x * _rms_scale(x) * g_ref[...]).astype(BF16)

    o_ref[...] = _dot(xn_ref[...], w_ref[...])


def _inproj(x, g, w):
    m, d = x.shape
    n = w.shape[1]
    tm = _row_tile(m, TILES["inproj_tm"])
    tn = _row_tile(n, TILES["inproj_tn"])
    return pl.pallas_call(
        _inproj_body,
        out_shape=jax.ShapeDtypeStruct((m, n), F32),
        grid=(m // tm, n // tn),
        in_specs=[
            pl.BlockSpec((tm, d), lambda i, j: (i, 0)),
            pl.BlockSpec((1, d), lambda i, j: (0, 0)),
            pl.BlockSpec((d, tn), lambda i, j: (0, j)),
        ],
        out_specs=pl.BlockSpec((tm, tn), lambda i, j: (i, j)),
        scratch_shapes=[pltpu.VMEM((tm, d), BF16)],
        compiler_params=_cparams(("parallel", "arbitrary")),
        name="inproj",
    )(x, g, w)


def _shift_rows(x, carry_rows):
    n = len(carry_rows)
    out = pltpu.roll(x, n, axis=0)
    row = lax.broadcasted_iota(jnp.int32, x.shape, 0)
    for i, c in enumerate(carry_rows):
        out = jnp.where(row == i, c, out)
    return out


def _prep_body(pb_ref, pc_ref, ph_ref, pr_ref, pk_ref, pv_ref, pl_ref, cprev_ref, sprev_ref,
               cw_ref, mu_ref, w0_ref, wdec_ref, a0_ref, wa_ref, wg_ref, kk_ref, ka_ref, rk_ref, bd_ref,
               yconv_ref, r_ref, lw_ref, k_ref, v_ref, kn_ref, b_ref, bonus_ref, g_ref, nconv_ref, nshift_ref,
               cu_scr, cs_scr):
    ti = pl.program_id(1)
    cc = pc_ref.shape[-1]
    rd = pr_ref.shape[-1]
    tt = pc_ref.shape[1]

    @pl.when(ti == 0)
    def _init():
        cu_scr[...] = cprev_ref[0]
        cs_scr[...] = sprev_ref[0]

    u = pc_ref[0] * ph_ref[0]
    c0 = cu_scr[0:1, :]
    c1 = cu_scr[1:2, :]
    u2 = _shift_rows(u, [c0, c1])
    u1 = _shift_rows(u, [c1])
    conv = u2 * cw_ref[0:1, :] + u1 * cw_ref[1:2, :] + u * cw_ref[2:3, :]
    yconv_ref[0] = pb_ref[0] * conv
    if tt >= 2:
        cu_scr[...] = u[tt - 2:tt, :]
    else:
        cu_scr[...] = jnp.concatenate([c1, u], axis=0)

    def shifted(p, lo, hi):
        prev = _shift_rows(p, [cs_scr[0:1, lo:hi]])
        q = p + (prev - p) * mu_ref[0:1, lo:hi]
        cs_scr[0:1, lo:hi] = p[tt - 1:tt, :]
        return q

    r = shifted(pr_ref[0], 0, rd)
    k = shifted(pk_ref[0], rd, 2 * rd)
    v = shifted(pv_ref[0], 2 * rd, 3 * rd)
    ql = shifted(pl_ref[0], 3 * rd, 3 * rd + LORA_BLOCK)

    ql_da = ql[:, 0:LANES]
    dec_in = w0_ref[...] + _dot(jnp.tanh(ql_da).astype(BF16), wdec_ref[...])
    z = -dec_in
    softplus = jnp.maximum(z, 0.0) + jnp.log(1.0 + jnp.exp(-jnp.abs(z)))
    w_log = -softplus - 0.5
    lw_ref[0] = -jnp.exp(w_log)
    a = jax.nn.sigmoid(a0_ref[...] + _dot(ql_da.astype(BF16), wa_ref[...]))
    g_ref[0] = _dot(jax.nn.sigmoid(ql[:, LANES:LANES + wg_ref.shape[0]]).astype(BF16), wg_ref[...])

    bd = bd_ref[...]
    kk = k * kk_ref[...]
    kk = kk / jnp.maximum(jnp.sqrt(_head_sum(kk * kk, bd)), L2_EPS)
    k = k * (1.0 + (a - 1.0) * ka_ref[...])
    r_ref[0] = r
    k_ref[0] = k
    v_ref[0] = v
    kn_ref[0] = kk
    b_ref[0] = kk * a
    bonus_ref[0] = _head_sum(r * k * rk_ref[...], bd) * v

    @pl.when(ti == pl.num_programs(1) - 1)
    def _fin():
        nconv_ref[0] = cu_scr[...]
        nshift_ref[0] = cs_scr[...]


def _prep(p3, conv_prev, shift_prev, wts):
    bsz, t, _ = p3.shape
    cc = conv_prev.shape[-1]
    rd = wts["w0"].shape[-1]
    sp = shift_prev.shape[-1]
    tt = _row_tile(t, TILES["prep_tt"])
    lora_blk = (3 * cc + 3 * rd) // LORA_BLOCK

    def col(width, idx):
        return pl.BlockSpec((1, tt, width), lambda b, i: (b, i, idx))

    def const(arr):
        return pl.BlockSpec(arr.shape, lambda b, i: (0,) * arr.ndim)

    names = ("conv_w", "mu", "w0", "wdec", "a0", "wa", "wg", "k_k", "k_a", "r_k", "bd")
    consts = [wts[n] for n in names]
    tok = jax.ShapeDtypeStruct((bsz, t, rd), F32)
    tok_spec = pl.BlockSpec((1, tt, rd), lambda b, i: (b, i, 0))
    return pl.pallas_call(
        _prep_body,
        out_shape=[jax.ShapeDtypeStruct((bsz, t, cc), F32)] + [tok] * 8
        + [jax.ShapeDtypeStruct((bsz, 2, cc), F32), jax.ShapeDtypeStruct((bsz, 1, sp), F32)],
        grid=(bsz, t // tt),
        in_specs=[col(cc, 0), col(cc, 1), col(cc, 2), col(rd, 3), col(rd, 4), col(rd, 5),
                  col(LORA_BLOCK, lora_blk),
                  pl.BlockSpec((1, 2, cc), lambda b, i: (b, 0, 0)),
                  pl.BlockSpec((1, 1, sp), lambda b, i: (b, 0, 0))]
        + [const(c) for c in consts],
        out_specs=[pl.BlockSpec((1, tt, cc), lambda b, i: (b, i, 0))] + [tok_spec] * 8
        + [pl.BlockSpec((1, 2, cc), lambda b, i: (b, 0, 0)), pl.BlockSpec((1, 1, sp), lambda b, i: (b, 0, 0))],
        scratch_shapes=[pltpu.VMEM((2, cc), F32), pltpu.VMEM((1, sp), F32)],
        compiler_params=_cparams(("parallel", "arbitrary")),
        name="prep",
    )(p3, p3, p3, p3, p3, p3, p3, conv_prev, shift_prev, *consts)


def _wkv_body(r_ref, lw_ref, k_ref, v_ref, kn_ref, b_ref, s0_ref, y_ref, sout_ref, s_scr, *, hpb, nchunk):
    ti = pl.program_id(2)
    c = WKV_CHUNK
    n = HEAD_DIM

    @pl.when(ti == 0)
    def _init():
        s_scr[...] = s0_ref[0]

    row = lax.broadcasted_iota(jnp.int32, (c, c), 0)
    col = lax.broadcasted_iota(jnp.int32, (c, c), 1)
    strict = row > col
    incl = row >= col
    ltri = jnp.where(incl, 1.0, 0.0).astype(BF16)

    def chunk(ci, carry):
        sl = pl.ds(pl.multiple_of(ci * c, c), c)
        lw = lw_ref[0, sl, :]
        l_hi = lw.astype(BF16)
        rem = lw - l_hi.astype(F32)
        l_mid = rem.astype(BF16)
        l_lo = (rem - l_mid.astype(F32)).astype(BF16)
        cum = _dot(ltri, l_hi) + (_dot(ltri, l_mid) + _dot(ltri, l_lo))
        cum_last = cum[c - 1:c, :]
        g_in = jnp.exp(cum)
        g_ex = jnp.exp(cum - lw)
        g_inv = jnp.exp(-cum)
        g_end = jnp.exp(cum_last - cum)
        g_last = jnp.exp(cum_last)
        rr = r_ref[0, sl, :]
        kk = k_ref[0, sl, :]
        vv = v_ref[0, sl, :]
        kn = kn_ref[0, sl, :]
        bb = b_ref[0, sl, :]
        a_all = kn * g_ex
        bm_all = bb * g_inv
        km_all = kk * g_inv
        p_all = rr * g_in
        kend_all = kk * g_end
        bend_all = bb * g_end
        ys = []
        for h in range(hpb):
            hs = slice(h * n, (h + 1) * n)
            s0 = s_scr[h]
            ap = jnp.concatenate([a_all[:, hs], p_all[:, hs]], axis=0)
            g_b = _mm3(ap, bm_all[:, hs], _dot_nt)
            g_k = _mm3(ap, km_all[:, hs], _dot_nt)
            a_s = _mm3(ap, s0, _dot_nt)
            v_h = vv[:, hs]
            m = jnp.where(strict, -g_b[:c], 0.0)
            w = a_s[:c] + _mm3(jnp.where(strict, g_k[:c], 0.0), v_h)
            w = w + _mm3(m, w)
            pm = m
            for _ in range(5):
                pm = _mm3(pm, pm)
                w = w + _mm3(pm, w)
            y = (a_s[c:] + _mm3(jnp.where(incl, g_k[c:], 0.0), v_h)) - _mm3(jnp.where(incl, g_b[c:], 0.0), w)
            ys.append(y)
            s_new = (s0 * g_last[:, hs] + _mm3(v_h, kend_all[:, hs], _dot_tn)) - _mm3(w, bend_all[:, hs], _dot_tn)
            s_scr[h] = s_new
        y_ref[0, sl, :] = jnp.concatenate(ys, axis=1) if hpb > 1 else ys[0]
        return carry

    lax.fori_loop(0, nchunk, chunk, 0)

    @pl.when(ti == pl.num_programs(2) - 1)
    def _fin():
        sout_ref[0] = s_scr[...]


def _wkv(r, lw, k, v, kn, b, s0):
    bsz, t, rd = r.shape
    nh = rd // HEAD_DIM
    tb = _row_tile(t, TILES["wkv_tb"])
    hpb = TILES["wkv_hpb"]
    wdt = hpb * HEAD_DIM
    tok = pl.BlockSpec((1, tb, wdt), lambda bi, hi, ti: (bi, ti, hi))
    st = pl.BlockSpec((1, hpb, HEAD_DIM, HEAD_DIM), lambda bi, hi, ti: (bi, hi, 0, 0))
    return pl.pallas_call(
        functools.partial(_wkv_body, hpb=hpb, nchunk=tb // WKV_CHUNK),
        out_shape=[jax.ShapeDtypeStruct((bsz, t, rd), F32),
                   jax.ShapeDtypeStruct((bsz, nh, HEAD_DIM, HEAD_DIM), F32)],
        grid=(bsz, nh // hpb, t // tb),
        in_specs=[tok] * 6 + [st],
        out_specs=[tok, st],
        scratch_shapes=[pltpu.VMEM((hpb, HEAD_DIM, HEAD_DIM), F32)],
        compiler_params=_cparams(("parallel", "parallel", "arbitrary")),
        name="wkv",
    )(r, lw, k, v, kn, b, s0)


def _mixout_body(h_ref, yc_ref, y_ref, bonus_ref, g_ref, lnw_ref, lnb_ref, bd_ref, wo_ref, gpost_ref, o_ref):
    cc = yc_ref.shape[-1]
    bd = bd_ref[...]
    y = y_ref[...]
    inv_n = 1.0 / HEAD_DIM
    mu = _head_sum(y, bd) * inv_n
    yc = y - mu
    var = _head_sum(yc * yc, bd) * inv_n
    yn = yc * lax.rsqrt(var + GN_EPS) * lnw_ref[...] + lnb_ref[...]
    y_rwkv = ((yn + bonus_ref[...]) * g_ref[...]).astype(BF16)
    m = _dot(yc_ref[...].astype(BF16), wo_ref[0:cc, :]) + _dot(y_rwkv, wo_ref[cc:, :])
    o_ref[...] = h_ref[...] + m * _rms_scale(m) * gpost_ref[...]


def _mixout(h, yc, y, bonus, g, lnw, lnb, bd, wo, g_post):
    m, d = h.shape
    cc = yc.shape[1]
    rd = y.shape[1]
    tm = _row_tile(m, TILES["mix_tm"])

    def rows(width):
        return pl.BlockSpec((tm, width), lambda i: (i, 0))

    def const(arr):
        return pl.BlockSpec(arr.shape, lambda i: (0,) * arr.ndim)

    return pl.pallas_call(
        _mixout_body,
        out_shape=jax.ShapeDtypeStruct((m, d), F32),
        grid=(m // tm,),
        in_specs=[rows(d), rows(cc), rows(rd), rows(rd), rows(rd),
                  const(lnw), const(lnb), const(bd), const(wo), const(g_post)],
        out_specs=rows(d),
        compiler_params=_cparams(("parallel",)),
        name="mixout",
    )(h, yc, y, bonus, g, lnw, lnb, bd, wo, g_post)


def _prepare_weights(g_ffn1_pre, w_ffn1_gate, w_ffn1_up, w_ffn1_down, g_ffn1_post, g_mix_pre, w_in, conv_w,
                     tshift_mu, w0, w_decay_up, a0, w_a_up, w_g_up, k_k, k_a, r_k, ln_x_w, ln_x_b, w_out,
                     g_mix_post, g_ffn2_pre, w_ffn2_gate, w_ffn2_up, w_ffn2_down, g_ffn2_post):
    d_ff = w_ffn1_gate.shape[1]
    f_pad = _round_up(d_ff, 512) - d_ff
    rd = w0.shape[-1]
    dl, al, gl = w_decay_up.shape[0], w_a_up.shape[0], w_g_up.shape[0]
    assert dl + al == LANES and gl <= LORA_BLOCK - LANES and rd % HEAD_DIM == 0
    lora = dl + al + gl
    gl_pad = _round_up(gl, LANES)

    def row(vec):
        return vec.reshape(1, -1).astype(F32)

    def ffn_w(wg, wu, wd):
        return (jnp.pad(wg.astype(BF16), ((0, 0), (0, f_pad))), jnp.pad(wu.astype(BF16), ((0, 0), (0, f_pad))),
                jnp.pad(wd.astype(BF16), ((0, f_pad), (0, 0))))

    head = jnp.arange(rd, dtype=jnp.int32) // HEAD_DIM
    return dict(
        ffn1=(row(g_ffn1_pre),) + ffn_w(w_ffn1_gate, w_ffn1_up, w_ffn1_down) + (row(g_ffn1_post),),
        ffn2=(row(g_ffn2_pre),) + ffn_w(w_ffn2_gate, w_ffn2_up, w_ffn2_down) + (row(g_ffn2_post),),
        g_mix_pre=row(g_mix_pre),
        w_in=jnp.pad(w_in.astype(BF16), ((0, 0), (0, LORA_BLOCK - lora))),
        lora=lora,
        conv_w=conv_w.astype(F32),
        mu=jnp.pad(row(tshift_mu), ((0, 0), (0, LORA_BLOCK - lora))),
        w0=row(w0),
        wdec=jnp.pad(w_decay_up.astype(BF16), ((0, al), (0, 0))),
        a0=row(a0),
        wa=jnp.pad(w_a_up.astype(BF16), ((dl, 0), (0, 0))),
        wg=jnp.pad(w_g_up.astype(BF16), ((0, gl_pad - gl), (0, 0))),
        k_k=row(k_k), k_a=row(k_a), r_k=row(r_k),
        bd=(head[:, None] == head[None, :]).astype(BF16),
        ln_w=row(ln_x_w), ln_b=row(ln_x_b),
        w_out=w_out.astype(BF16),
        g_mix_post=row(g_mix_post),
    )


def _layer(x3, conv_prev, shift_prev, wkv_prev, wts):
    bsz, t, d = x3.shape
    m = bsz * t
    lora = wts["lora"]
    x = x3.reshape(m, d)
    h = _ffn(x, *wts["ffn1"])
    p = _inproj(h, wts["g_mix_pre"], wts["w_in"])
    shift_pad = jnp.pad(shift_prev.astype(F32), ((0, 0), (0, 0), (0, LORA_BLOCK - lora)))
    yc, r, lw, k, v, kn, b, bonus, g, new_conv, new_shift = _prep(
        p.reshape(bsz, t, -1), conv_prev.astype(F32), shift_pad, wts)
    y, new_wkv = _wkv(r, lw, k, v, kn, b, wkv_prev.astype(F32))
    rd = r.shape[-1]
    h = _mixout(h, yc.reshape(m, -1), y.reshape(m, rd), bonus.reshape(m, rd), g.reshape(m, rd),
                wts["ln_w"], wts["ln_b"], wts["bd"], wts["w_out"], wts["g_mix_post"])
    out = _ffn(h, *wts["ffn2"])
    shift_dim = shift_prev.shape[-1]
    return out.reshape(bsz, t, d), new_conv, new_shift[:, :, :shift_dim], new_wkv


def kernel(x_prompt, x_sample, cache_conv, cache_shift, state_wkv, g_ffn1_pre, w_ffn1_gate, w_ffn1_up, w_ffn1_down, g_ffn1_post, g_mix_pre, w_in, conv_w, tshift_mu, w0, w_decay_up, a0, w_a_up, w_g_up, k_k, k_a, r_k, ln_x_w, ln_x_b, w_out, g_mix_post, g_ffn2_pre, w_ffn2_gate, w_ffn2_up, w_ffn2_down, g_ffn2_post):
    depth = w_in.shape[0]
    bp = x_prompt.shape[0]
    conv_rows, conv_ch = cache_conv.shape[2], cache_conv.shape[3]
    shift_dim = cache_shift.shape[-1]
    nh, hd = state_wkv.shape[2], state_wkv.shape[3]
    per_layer = (g_ffn1_pre, w_ffn1_gate, w_ffn1_up, w_ffn1_down, g_ffn1_post, g_mix_pre, w_in, conv_w,
                 tshift_mu, w0, w_decay_up, a0, w_a_up, w_g_up, k_k, k_a, r_k, ln_x_w, ln_x_b, w_out,
                 g_mix_post, g_ffn2_pre, w_ffn2_gate, w_ffn2_up, w_ffn2_down, g_ffn2_post)
    yp, ys = x_prompt, x_sample
    outs = [[] for _ in range(6)]
    for l in range(depth):
        wts = _prepare_weights(*(w[l] for w in per_layer))
        zc = jnp.zeros((bp, conv_rows, conv_ch), F32)
        zs = jnp.zeros((bp, 1, shift_dim), F32)
        zw = jnp.zeros((bp, nh, hd, hd), F32)
        yp, c1, s1, w1 = _layer(yp, zc, zs, zw, wts)
        ys, c2, s2, w2 = _layer(ys, cache_conv[l], cache_shift[l], state_wkv[l], wts)
        for lst, val, ref in zip(outs, (c1, s1, w1, c2, s2, w2),
                                 (cache_conv, cache_shift, state_wkv, cache_conv, cache_shift, state_wkv)):
            lst.append(val.astype(ref.dtype))
    return (yp, ys) + tuple(jnp.stack(o, 0) for o in outs)
```

```python
import functools

import jax
import jax.numpy as jnp
from jax import lax
from jax.experimental import pallas as pl
from jax.experimental.pallas import tpu as pltpu

F32 = jnp.float32
BF16 = jnp.bfloat16

RMS_EPS = 1e-6
GN_EPS = 64e-5
L2_EPS = 1e-12
HEAD_DIM = 64
WKV_CHUNK = 64
MXU_DIM = 256
WKV_GROUP = MXU_DIM // HEAD_DIM
LANES = 128
LORA_BLOCK = 512
VMEM_LIMIT = 56 * 1024 * 1024
TILES = dict(ffn_tm=512, ffn_tf=512, inproj_tm=512, inproj_tn=512, prep_tt=256, wkv_tb=512, wkv_hpb=16, mix_tm=256)
PASSES_GRAM = 1
PASSES_STATE = 1
PASSES_SOLVE = 1


def _cparams(sem):
    return pltpu.CompilerParams(dimension_semantics=sem, vmem_limit_bytes=VMEM_LIMIT)


def _round_up(n, m):
    return (n + m - 1) // m * m


def _row_tile(m, pref):
    return pref if m % pref == 0 else m


def _rms_scale(x):
    return lax.rsqrt(jnp.mean(x * x, axis=-1, keepdims=True) + RMS_EPS)


def _dot(a, b):
    return jnp.dot(a, b, preferred_element_type=F32)


def _dot_nt(a, b):
    return lax.dot_general(a, b, (((1,), (1,)), ((), ())), preferred_element_type=F32)


def _dot_tn(a, b):
    return lax.dot_general(a, b, (((0,), (0,)), ((), ())), preferred_element_type=F32)


def _split2(x):
    hi = x.astype(BF16)
    lo = (x - hi.astype(F32)).astype(BF16)
    return hi, lo


def _mm3(a, b, dot=_dot):
    ah, al = _split2(a)
    bh, bl = _split2(b)
    return dot(ah, bh) + (dot(ah, bl) + dot(al, bh))


def _head_sum(x, bd):
    hi, lo = _split2(x)
    return _dot(hi, bd) + _dot(lo, bd)


def _ffn_body(x_ref, gpre_ref, wg_ref, wu_ref, wd_ref, gpost_ref, o_ref, xn_ref, acc_ref):
    j = pl.program_id(1)

    @pl.when(j == 0)
    def _init():
        x = x_ref[...]
        xn_ref[...] = (x * _rms_scale(x) * gpre_ref[...]).astype(BF16)
        acc_ref[...] = jnp.zeros_like(acc_ref)

    xn = xn_ref[...]
    gate = _dot(xn, wg_ref[...])
    up = _dot(xn, wu_ref[...])
    hmid = (gate * jax.nn.sigmoid(gate) * up).astype(BF16)
    acc_ref[...] += _dot(hmid, wd_ref[...])

    @pl.when(j == pl.num_programs(1) - 1)
    def _fin():
        a = acc_ref[...]
        o_ref[...] = x_ref[...] + 0.5 * (a * _rms_scale(a) * gpost_ref[...])


def _ffn(x, g_pre, wg, wu, wd, g_post):
    m, d = x.shape
    f = wg.shape[1]
    tm = _row_tile(m, TILES["ffn_tm"])
    tf = _row_tile(f, TILES["ffn_tf"])
    return pl.pallas_call(
        _ffn_body,
        out_shape=jax.ShapeDtypeStruct((m, d), F32),
        grid=(m // tm, f // tf),
        in_specs=[
            pl.BlockSpec((tm, d), lambda i, j: (i, 0)),
            pl.BlockSpec((1, d), lambda i, j: (0, 0)),
            pl.BlockSpec((d, tf), lambda i, j: (0, j)),
            pl.BlockSpec((d, tf), lambda i, j: (0, j)),
            pl.BlockSpec((tf, d), lambda i, j: (j, 0)),
            pl.BlockSpec((1, d), lambda i, j: (0, 0)),
        ],
        out_specs=pl.BlockSpec((tm, d), lambda i, j: (i, 0)),
        scratch_shapes=[pltpu.VMEM((tm, d), BF16), pltpu.VMEM((tm, d), F32)],
        compiler_params=_cparams(("parallel", "arbitrary")),
        name="ffn",
    )(x, g_pre, wg, wu, wd, g_post)


def _inproj_body(x_ref, g_ref, w_ref, o_ref, xn_ref):
    @pl.when(pl.program_id(1) == 0)
    def _init():
        x = x_ref[...]
        xn_ref[...] = (x * _rms_scale(x) * g_ref[...]).astype(BF16)

    o_ref[...] = _dot(xn_ref[...], w_ref[...])


def _inproj(x, g, w):
    m, d = x.shape
    n = w.shape[1]
    tm = _row_tile(m, TILES["inproj_tm"])
    tn = _row_tile(n, TILES["inproj_tn"])
    return pl.pallas_call(
        _inproj_body,
        out_shape=jax.ShapeDtypeStruct((m, n), F32),
        grid=(m // tm, n // tn),
        in_specs=[
            pl.BlockSpec((tm, d), lambda i, j: (i, 0)),
            pl.BlockSpec((1, d), lambda i, j: (0, 0)),
            pl.BlockSpec((d, tn), lambda i, j: (0, j)),
        ],
        out_specs=pl.BlockSpec((tm, tn), lambda i, j: (i, j)),
        scratch_shapes=[pltpu.VMEM((tm, d), BF16)],
        compiler_params=_cparams(("parallel", "arbitrary")),
        name="inproj",
    )(x, g, w)


def _shift_rows(x, carry_rows):
    n = len(carry_rows)
    out = pltpu.roll(x, n, axis=0)
    row = lax.broadcasted_iota(jnp.int32, x.shape, 0)
    for i, c in enumerate(carry_rows):
        out = jnp.where(row == i, c, out)
    return out


def _prep_body(pb_ref, pc_ref, ph_ref, pr_ref, pk_ref, pv_ref, pl_ref, cprev_ref, sprev_ref,
               cw_ref, mu_ref, w0_ref, wdec_ref, a0_ref, wa_ref, wg_ref, kk_ref, ka_ref, rk_ref, bd_ref,
               yconv_ref, r_ref, lw_ref, k_ref, v_ref, kn_ref, b_ref, bonus_ref, g_ref, nconv_ref, nshift_ref,
               cu_scr, cs_scr):
    ti = pl.program_id(1)
    cc = pc_ref.shape[-1]
    rd = pr_ref.shape[-1]
    tt = pc_ref.shape[1]

    @pl.when(ti == 0)
    def _init():
        cu_scr[...] = cprev_ref[0]
        cs_scr[...] = sprev_ref[0]

    u = pc_ref[0] * ph_ref[0]
    c0 = cu_scr[0:1, :]
    c1 = cu_scr[1:2, :]
    u2 = _shift_rows(u, [c0, c1])
    u1 = _shift_rows(u, [c1])
    conv = u2 * cw_ref[0:1, :] + u1 * cw_ref[1:2, :] + u * cw_ref[2:3, :]
    yconv_ref[0] = pb_ref[0] * conv
    if tt >= 2:
        cu_scr[...] = u[tt - 2:tt, :]
    else:
        cu_scr[...] = jnp.concatenate([c1, u], axis=0)

    def shifted(p, lo, hi):
        prev = _shift_rows(p, [cs_scr[0:1, lo:hi]])
        q = p + (prev - p) * mu_ref[0:1, lo:hi]
        cs_scr[0:1, lo:hi] = p[tt - 1:tt, :]
        return q

    r = shifted(pr_ref[0], 0, rd)
    k = shifted(pk_ref[0], rd, 2 * rd)
    v = shifted(pv_ref[0], 2 * rd, 3 * rd)
    ql = shifted(pl_ref[0], 3 * rd, 3 * rd + LORA_BLOCK)

    ql_da = ql[:, 0:LANES]
    dec_in = w0_ref[...] + _dot(jnp.tanh(ql_da).astype(BF16), wdec_ref[...])
    z = -dec_in
    softplus = jnp.maximum(z, 0.0) + jnp.log(1.0 + jnp.exp(-jnp.abs(z)))
    w_log = -softplus - 0.5
    lw_ref[0] = -jnp.exp(w_log)
    a = jax.nn.sigmoid(a0_ref[...] + _dot(ql_da.astype(BF16), wa_ref[...]))
    g_ref[0] = _dot(jax.nn.sigmoid(ql[:, LANES:LANES + wg_ref.shape[0]]).astype(BF16), wg_ref[...])

    bd = bd_ref[...]
    kk = k * kk_ref[...]
    kk = kk / jnp.maximum(jnp.sqrt(_head_sum(kk * kk, bd)), L2_EPS)
    k = k * (1.0 + (a - 1.0) * ka_ref[...])
    r_ref[0] = r
    k_ref[0] = k
    v_ref[0] = v
    kn_ref[0] = kk
    b_ref[0] = kk * a
    bonus_ref[0] = _head_sum(r * k * rk_ref[...], bd) * v

    @pl.when(ti == pl.num_programs(1) - 1)
    def _fin():
        nconv_ref[0] = cu_scr[...]
        nshift_ref[0] = cs_scr[...]


def _prep(p3, conv_prev, shift_prev, wts):
    bsz, t, _ = p3.shape
    cc = conv_prev.shape[-1]
    rd = wts["w0"].shape[-1]
    sp = shift_prev.shape[-1]
    tt = _row_tile(t, TILES["prep_tt"])
    lora_blk = (3 * cc + 3 * rd) // LORA_BLOCK

    def col(width, idx):
        return pl.BlockSpec((1, tt, width), lambda b, i: (b, i, idx))

    def const(arr):
        return pl.BlockSpec(arr.shape, lambda b, i: (0,) * arr.ndim)

    names = ("conv_w", "mu", "w0", "wdec", "a0", "wa", "wg", "k_k", "k_a", "r_k", "bd")
    consts = [wts[n] for n in names]
    tok = jax.ShapeDtypeStruct((bsz, t, rd), F32)
    tok_spec = pl.BlockSpec((1, tt, rd), lambda b, i: (b, i, 0))
    return pl.pallas_call(
        _prep_body,
        out_shape=[jax.ShapeDtypeStruct((bsz, t, cc), F32)] + [tok] * 8
        + [jax.ShapeDtypeStruct((bsz, 2, cc), F32), jax.ShapeDtypeStruct((bsz, 1, sp), F32)],
        grid=(bsz, t // tt),
        in_specs=[col(cc, 0), col(cc, 1), col(cc, 2), col(rd, 3), col(rd, 4), col(rd, 5),
                  col(LORA_BLOCK, lora_blk),
                  pl.BlockSpec((1, 2, cc), lambda b, i: (b, 0, 0)),
                  pl.BlockSpec((1, 1, sp), lambda b, i: (b, 0, 0))]
        + [const(c) for c in consts],
        out_specs=[pl.BlockSpec((1, tt, cc), lambda b, i: (b, i, 0))] + [tok_spec] * 8
        + [pl.BlockSpec((1, 2, cc), lambda b, i: (b, 0, 0)), pl.BlockSpec((1, 1, sp), lambda b, i: (b, 0, 0))],
        scratch_shapes=[pltpu.VMEM((2, cc), F32), pltpu.VMEM((1, sp), F32)],
        compiler_params=_cparams(("parallel", "arbitrary")),
        name="prep",
    )(p3, p3, p3, p3, p3, p3, p3, conv_prev, shift_prev, *consts)


def _bd(y, mask):
    reps = mask.shape[0] // y.shape[0]
    return jnp.where(mask, jnp.concatenate([y] * reps, axis=0), jnp.zeros((), y.dtype))


def _mm_heads(x, y, mask, passes, nt=False):
    dot = _dot_nt if nt else _dot
    if passes == 1:
        return dot(x.astype(BF16), _bd(y.astype(BF16), mask))
    xh, xl = _split2(x)
    yh, yl = _split2(y)
    bh = _bd(yh, mask)
    return dot(xh, bh) + (dot(xh, _bd(yl, mask)) + dot(xl, bh))


def _wkv_body(r_ref, lw_ref, k_ref, v_ref, kn_ref, b_ref, s0_ref, y_ref, sout_ref, s_scr, *, nchunk):
    ti = pl.program_id(2)
    c = WKV_CHUNK
    n = HEAD_DIM
    wdt = WKV_GROUP * n
    hpb = WKV_GROUP
    ngroups = s_scr.shape[-1] // wdt

    @pl.when(ti == 0)
    def _init():
        s_scr[...] = s0_ref[0]

    row = lax.broadcasted_iota(jnp.int32, (c, c), 0)
    col = lax.broadcasted_iota(jnp.int32, (c, c), 1)
    ltri = jnp.where(row >= col, 1.0, 0.0).astype(BF16)
    prow = lax.broadcasted_iota(jnp.int32, (c, wdt), 0)
    pcol = lax.broadcasted_iota(jnp.int32, (c, wdt), 1) % n
    strict = prow > pcol
    incl = prow >= pcol
    brow = lax.broadcasted_iota(jnp.int32, (wdt, wdt), 0) // n
    bcol = lax.broadcasted_iota(jnp.int32, (wdt, wdt), 1) // n
    same_head = brow == bcol

    def group_chunk(sl, ln):
        lw = lw_ref[0, sl, ln]
        l_hi = lw.astype(BF16)
        rem = lw - l_hi.astype(F32)
        l_mid = rem.astype(BF16)
        l_lo = (rem - l_mid.astype(F32)).astype(BF16)
        cum = _dot(ltri, l_hi) + (_dot(ltri, l_mid) + _dot(ltri, l_lo))
        yield
        cum_last = cum[c - 1:c, :]
        g_end = jnp.exp(cum_last - cum)
        g_inv = jnp.exp(-cum)
        kk = k_ref[0, sl, ln]
        vv = v_ref[0, sl, ln]
        bb = b_ref[0, sl, ln]
        s0 = s_scr[:, ln]
        ap = jnp.concatenate([kn_ref[0, sl, ln] * jnp.exp(cum - lw), r_ref[0, sl, ln] * jnp.exp(cum)], axis=0)
        g_b = _mm_heads(ap, bb * g_inv, same_head, PASSES_GRAM, nt=True)
        g_k = _mm_heads(ap, kk * g_inv, same_head, PASSES_GRAM, nt=True)
        a_s = _mm_heads(ap, s0, same_head, PASSES_STATE, nt=True)
        yield
        m = jnp.where(strict, -g_b[:c], 0.0)
        lk = jnp.concatenate([jnp.where(strict, g_k[:c], 0.0), jnp.where(incl, g_k[c:], 0.0)], axis=0)
        t1 = a_s + _mm_heads(lk, vv, same_head, PASSES_GRAM)
        pm = _mm_heads(m, m, same_head, PASSES_SOLVE)
        yield
        w = t1[:c]
        w = w + _mm_heads(m, w, same_head, PASSES_SOLVE)
        yield
        for j in range(5):
            w = w + _mm_heads(pm, w, same_head, PASSES_SOLVE)
            if j < 4:
                pm = _mm_heads(pm, pm, same_head, PASSES_SOLVE)
            yield
        y_ref[0, sl, ln] = t1[c:] - _mm_heads(jnp.where(incl, g_b[c:], 0.0), w, same_head, PASSES_GRAM)
        vu = jnp.concatenate([vv, -w], axis=0)
        kb = jnp.concatenate([kk * g_end, bb * g_end], axis=0)
        if PASSES_STATE == 1:
            gram = _dot_tn(vu.astype(BF16), kb.astype(BF16))
        else:
            gram = _mm3(vu, kb, _dot_tn)
        gram = jnp.where(same_head, gram, 0.0)
        upd = gram[0:n]
        for h in range(1, hpb):
            upd = upd + gram[h * n:(h + 1) * n]
        s_scr[:, ln] = s0 * jnp.exp(cum_last) + upd

    def chunk(ci, carry):
        sl = pl.ds(pl.multiple_of(ci * c, c), c)
        live = [group_chunk(sl, slice(g * wdt, (g + 1) * wdt)) for g in range(ngroups)]
        while live:
            live = [gen for gen in live if next(gen, live) is not live]
        return carry

    lax.fori_loop(0, nchunk, chunk, 0)

    @pl.when(ti == pl.num_programs(2) - 1)
    def _fin():
        sout_ref[0] = s_scr[...]


def _wkv(r, lw, k, v, kn, b, s0):
    bsz, t, rd = r.shape
    tb = _row_tile(t, TILES["wkv_tb"])
    wdt = TILES["wkv_hpb"] * HEAD_DIM
    tok = pl.BlockSpec((1, tb, wdt), lambda bi, hi, ti: (bi, ti, hi))
    st = pl.BlockSpec((1, HEAD_DIM, wdt), lambda bi, hi, ti: (bi, 0, hi))
    return pl.pallas_call(
        functools.partial(_wkv_body, nchunk=tb // WKV_CHUNK),
        out_shape=[jax.ShapeDtypeStruct((bsz, t, rd), F32),
                   jax.ShapeDtypeStruct((bsz, HEAD_DIM, rd), F32)],
        grid=(bsz, rd // wdt, t // tb),
        in_specs=[tok] * 6 + [st],
        out_specs=[tok, st],
        scratch_shapes=[pltpu.VMEM((HEAD_DIM, wdt), F32)],
        compiler_params=_cparams(("parallel", "parallel", "arbitrary")),
        name="wkv",
    )(r, lw, k, v, kn, b, s0)


def _mixout_body(h_ref, yc_ref, y_ref, bonus_ref, g_ref, lnw_ref, lnb_ref, bd_ref, wo_ref, gpost_ref, o_ref):
    cc = yc_ref.shape[-1]
    bd = bd_ref[...]
    y = y_ref[...]
    inv_n = 1.0 / HEAD_DIM
    mu = _head_sum(y, bd) * inv_n
    yc = y - mu
    var = _head_sum(yc * yc, bd) * inv_n
    yn = yc * lax.rsqrt(var + GN_EPS) * lnw_ref[...] + lnb_ref[...]
    y_rwkv = ((yn + bonus_ref[...]) * g_ref[...]).astype(BF16)
    m = _dot(yc_ref[...].astype(BF16), wo_ref[0:cc, :]) + _dot(y_rwkv, wo_ref[cc:, :])
    o_ref[...] = h_ref[...] + m * _rms_scale(m) * gpost_ref[...]


def _mixout(h, yc, y, bonus, g, lnw, lnb, bd, wo, g_post):
    m, d = h.shape
    cc = yc.shape[1]
    rd = y.shape[1]
    tm = _row_tile(m, TILES["mix_tm"])

    def rows(width):
        return pl.BlockSpec((tm, width), lambda i: (i, 0))

    def const(arr):
        return pl.BlockSpec(arr.shape, lambda i: (0,) * arr.ndim)

    return pl.pallas_call(
        _mixout_body,
        out_shape=jax.ShapeDtypeStruct((m, d), F32),
        grid=(m // tm,),
        in_specs=[rows(d), rows(cc), rows(rd), rows(rd), rows(rd),
                  const(lnw), const(lnb), const(bd), const(wo), const(g_post)],
        out_specs=rows(d),
        compiler_params=_cparams(("parallel",)),
        name="mixout",
    )(h, yc, y, bonus, g, lnw, lnb, bd, wo, g_post)


def _prepare_weights(g_ffn1_pre, w_ffn1_gate, w_ffn1_up, w_ffn1_down, g_ffn1_post, g_mix_pre, w_in, conv_w,
                     tshift_mu, w0, w_decay_up, a0, w_a_up, w_g_up, k_k, k_a, r_k, ln_x_w, ln_x_b, w_out,
                     g_mix_post, g_ffn2_pre, w_ffn2_gate, w_ffn2_up, w_ffn2_down, g_ffn2_post):
    d_ff = w_ffn1_gate.shape[1]
    f_pad = _round_up(d_ff, 512) - d_ff
    rd = w0.shape[-1]
    dl, al, gl = w_decay_up.shape[0], w_a_up.shape[0], w_g_up.shape[0]
    assert dl + al == LANES and gl <= LORA_BLOCK - LANES and rd % HEAD_DIM == 0
    lora = dl + al + gl
    gl_pad = _round_up(gl, LANES)

    def row(vec):
        return vec.reshape(1, -1).astype(F32)

    def ffn_w(wg, wu, wd):
        return (jnp.pad(wg.astype(BF16), ((0, 0), (0, f_pad))), jnp.pad(wu.astype(BF16), ((0, 0), (0, f_pad))),
                jnp.pad(wd.astype(BF16), ((0, f_pad), (0, 0))))

    head = jnp.arange(rd, dtype=jnp.int32) // HEAD_DIM
    return dict(
        ffn1=(row(g_ffn1_pre),) + ffn_w(w_ffn1_gate, w_ffn1_up, w_ffn1_down) + (row(g_ffn1_post),),
        ffn2=(row(g_ffn2_pre),) + ffn_w(w_ffn2_gate, w_ffn2_up, w_ffn2_down) + (row(g_ffn2_post),),
        g_mix_pre=row(g_mix_pre),
        w_in=jnp.pad(w_in.astype(BF16), ((0, 0), (0, LORA_BLOCK - lora))),
        lora=lora,
        conv_w=conv_w.astype(F32),
        mu=jnp.pad(row(tshift_mu), ((0, 0), (0, LORA_BLOCK - lora))),
        w0=row(w0),
        wdec=jnp.pad(w_decay_up.astype(BF16), ((0, al), (0, 0))),
        a0=row(a0),
        wa=jnp.pad(w_a_up.astype(BF16), ((dl, 0), (0, 0))),
        wg=jnp.pad(w_g_up.astype(BF16), ((0, gl_pad - gl), (0, 0))),
        k_k=row(k_k), k_a=row(k_a), r_k=row(r_k),
        bd=(head[:, None] == head[None, :]).astype(BF16),
        ln_w=row(ln_x_w), ln_b=row(ln_x_b),
        w_out=w_out.astype(BF16),
        g_mix_post=row(g_mix_post),
    )


def _layer(x3, conv_prev, shift_prev, wkv_prev, wts):
    bsz, t, d = x3.shape
    m = bsz * t
    lora = wts["lora"]
    x = x3.reshape(m, d)
    h = _ffn(x, *wts["ffn1"])
    p = _inproj(h, wts["g_mix_pre"], wts["w_in"])
    shift_pad = jnp.pad(shift_prev.astype(F32), ((0, 0), (0, 0), (0, LORA_BLOCK - lora)))
    yc, r, lw, k, v, kn, b, bonus, g, new_conv, new_shift = _prep(
        p.reshape(bsz, t, -1), conv_prev.astype(F32), shift_pad, wts)
    rd = r.shape[-1]
    nh, hd = wkv_prev.shape[1], wkv_prev.shape[2]
    s0 = wkv_prev.astype(F32).transpose(0, 2, 1, 3).reshape(bsz, hd, rd)
    y, s_new = _wkv(r, lw, k, v, kn, b, s0)
    new_wkv = s_new.reshape(bsz, hd, nh, hd).transpose(0, 2, 1, 3)
    h = _mixout(h, yc.reshape(m, -1), y.reshape(m, rd), bonus.reshape(m, rd), g.reshape(m, rd),
                wts["ln_w"], wts["ln_b"], wts["bd"], wts["w_out"], wts["g_mix_post"])
    out = _ffn(h, *wts["ffn2"])
    shift_dim = shift_prev.shape[-1]
    return out.reshape(bsz, t, d), new_conv, new_shift[:, :, :shift_dim], new_wkv


def kernel(x_prompt, x_sample, cache_conv, cache_shift, state_wkv, g_ffn1_pre, w_ffn1_gate, w_ffn1_up, w_ffn1_down, g_ffn1_post, g_mix_pre, w_in, conv_w, tshift_mu, w0, w_decay_up, a0, w_a_up, w_g_up, k_k, k_a, r_k, ln_x_w, ln_x_b, w_out, g_mix_post, g_ffn2_pre, w_ffn2_gate, w_ffn2_up, w_ffn2_down, g_ffn2_post):
    depth = w_in.shape[0]
    bp = x_prompt.shape[0]
    conv_rows, conv_ch = cache_conv.shape[2], cache_conv.shape[3]
    shift_dim = cache_shift.shape[-1]
    nh, hd = state_wkv.shape[2], state_wkv.shape[3]
    per_layer = (g_ffn1_pre, w_ffn1_gate, w_ffn1_up, w_ffn1_down, g_ffn1_post, g_mix_pre, w_in, conv_w,
                 tshift_mu, w0, w_decay_up, a0, w_a_up, w_g_up, k_k, k_a, r_k, ln_x_w, ln_x_b, w_out,
                 g_mix_post, g_ffn2_pre, w_ffn2_gate, w_ffn2_up, w_ffn2_down, g_ffn2_post)
    yp, ys = x_prompt, x_sample
    outs = [[] for _ in range(6)]
    for l in range(depth):
        wts = _prepare_weights(*(w[l] for w in per_layer))
        zc = jnp.zeros((bp, conv_rows, conv_ch), F32)
        zs = jnp.zeros((bp, 1, shift_dim), F32)
        zw = jnp.zeros((bp, nh, hd, hd), F32)
        yp, c1, s1, w1 = _layer(yp, zc, zs, zw, wts)
        ys, c2, s2, w2 = _layer(ys, cache_conv[l], cache_shift[l], state_wkv[l], wts)
        for lst, val, ref in zip(outs, (c1, s1, w1, c2, s2, w2),
                                 (cache_conv, cache_shift, state_wkv, cache_conv, cache_shift, state_wkv)):
            lst.append(val.astype(ref.dtype))
    return (yp, ys) + tuple(jnp.stack(o, 0) for o in outs)
```

```python
import functools

import jax
import jax.numpy as jnp
from jax import lax
from jax.experimental import pallas as pl
from jax.experimental.pallas import tpu as pltpu

F32 = jnp.float32
BF16 = jnp.bfloat16

RMS_EPS = 1e-6
GN_EPS = 64e-5
L2_EPS = 1e-12
HEAD_DIM = 64
WKV_CHUNK = 64
MXU_DIM = 256
WKV_GROUP = MXU_DIM // HEAD_DIM
LANES = 128
LORA_BLOCK = 512
VMEM_LIMIT = 56 * 1024 * 1024
TILES = dict(ffn_tm=512, ffn_tf=512, inproj_tm=512, inproj_tn=1664, prep_tt=256, wkv_tb=512, wkv_hpb=16, mix_tm=256)
RESIDENT_ROWS = 1024
PASSES_GRAM = 1
PASSES_STATE = 1
PASSES_SOLVE = 1


def _cparams(sem):
    return pltpu.CompilerParams(dimension_semantics=sem, vmem_limit_bytes=VMEM_LIMIT)


def _round_up(n, m):
    return (n + m - 1) // m * m


def _row_tile(m, pref):
    return pref if m % pref == 0 else m


def _rows_plan(m, pref):
    if m <= RESIDENT_ROWS or m % pref:
        return m, pl.Buffered(1)
    return pref, None


def _rms_scale(x):
    return lax.rsqrt(jnp.mean(x * x, axis=-1, keepdims=True) + RMS_EPS)


def _dot(a, b):
    return jnp.dot(a, b, preferred_element_type=F32)


def _dot_nt(a, b):
    return lax.dot_general(a, b, (((1,), (1,)), ((), ())), preferred_element_type=F32)


def _dot_tn(a, b):
    return lax.dot_general(a, b, (((0,), (0,)), ((), ())), preferred_element_type=F32)


def _split2(x):
    hi = x.astype(BF16)
    lo = (x - hi.astype(F32)).astype(BF16)
    return hi, lo


def _mm3(a, b, dot=_dot):
    ah, al = _split2(a)
    bh, bl = _split2(b)
    return dot(ah, bh) + (dot(ah, bl) + dot(al, bh))


def _head_sum(x, bd):
    hi, lo = _split2(x)
    return _dot(hi, bd) + _dot(lo, bd)


def _ffn_body(x_ref, gpre_ref, wg_ref, wu_ref, wd_ref, gpost_ref, o_ref, xn_ref, acc_ref):
    j = pl.program_id(1)

    @pl.when(j == 0)
    def _init():
        x = x_ref[...]
        xn_ref[...] = (x * _rms_scale(x) * gpre_ref[...]).astype(BF16)
        acc_ref[...] = jnp.zeros_like(acc_ref)

    xn = xn_ref[...]
    gate = _dot(xn, wg_ref[...])
    up = _dot(xn, wu_ref[...])
    hmid = (gate * jax.nn.sigmoid(gate) * up).astype(BF16)
    acc_ref[...] += _dot(hmid, wd_ref[...])

    @pl.when(j == pl.num_programs(1) - 1)
    def _fin():
        a = acc_ref[...]
        o_ref[...] = x_ref[...] + 0.5 * (a * _rms_scale(a) * gpost_ref[...])


def _ffn(x, g_pre, wg, wu, wd, g_post):
    m, d = x.shape
    f = wg.shape[1]
    tm, rows_mode = _rows_plan(m, TILES["ffn_tm"])
    tf = _row_tile(f, TILES["ffn_tf"])
    return pl.pallas_call(
        _ffn_body,
        out_shape=jax.ShapeDtypeStruct((m, d), F32),
        grid=(m // tm, f // tf),
        in_specs=[
            pl.BlockSpec((tm, d), lambda i, j: (i, 0), pipeline_mode=rows_mode),
            pl.BlockSpec((1, d), lambda i, j: (0, 0)),
            pl.BlockSpec((d, tf), lambda i, j: (0, j)),
            pl.BlockSpec((d, tf), lambda i, j: (0, j)),
            pl.BlockSpec((tf, d), lambda i, j: (j, 0)),
            pl.BlockSpec((1, d), lambda i, j: (0, 0)),
        ],
        out_specs=pl.BlockSpec((tm, d), lambda i, j: (i, 0), pipeline_mode=rows_mode),
        scratch_shapes=[pltpu.VMEM((tm, d), BF16), pltpu.VMEM((tm, d), F32)],
        compiler_params=_cparams(("parallel", "arbitrary")),
        name="ffn",
    )(x, g_pre, wg, wu, wd, g_post)


def _inproj_body(x_ref, g_ref, w_ref, o_ref, xn_ref):
    @pl.when(pl.program_id(1) == 0)
    def _init():
        x = x_ref[...]
        xn_ref[...] = (x * _rms_scale(x) * g_ref[...]).astype(BF16)

    o_ref[...] = _dot(xn_ref[...], w_ref[...])


def _inproj(x, g, w):
    m, d = x.shape
    n = w.shape[1]
    tm, rows_mode = _rows_plan(m, TILES["inproj_tm"])
    tn = _row_tile(n, TILES["inproj_tn"])
    return pl.pallas_call(
        _inproj_body,
        out_shape=jax.ShapeDtypeStruct((m, n), F32),
        grid=(m // tm, n // tn),
        in_specs=[
            pl.BlockSpec((tm, d), lambda i, j: (i, 0), pipeline_mode=rows_mode),
            pl.BlockSpec((1, d), lambda i, j: (0, 0)),
            pl.BlockSpec((d, tn), lambda i, j: (0, j)),
        ],
        out_specs=pl.BlockSpec((tm, tn), lambda i, j: (i, j)),
        scratch_shapes=[pltpu.VMEM((tm, d), BF16)],
        compiler_params=_cparams(("parallel", "arbitrary")),
        name="inproj",
    )(x, g, w)


def _shift_rows(x, carry_rows):
    n = len(carry_rows)
    out = pltpu.roll(x, n, axis=0)
    row = lax.broadcasted_iota(jnp.int32, x.shape, 0)
    for i, c in enumerate(carry_rows):
        out = jnp.where(row == i, c, out)
    return out


def _prep_body(pb_ref, pc_ref, ph_ref, pr_ref, pk_ref, pv_ref, pl_ref, cprev_ref, sprev_ref,
               cw_ref, mu_ref, w0_ref, wdec_ref, a0_ref, wa_ref, wg_ref, kk_ref, ka_ref, rk_ref, bd_ref,
               yconv_ref, r_ref, lw_ref, k_ref, v_ref, kn_ref, b_ref, bonus_ref, g_ref, nconv_ref, nshift_ref,
               cu_scr, cs_scr):
    ti = pl.program_id(1)
    cc = pc_ref.shape[-1]
    rd = pr_ref.shape[-1]
    tt = pc_ref.shape[1]

    @pl.when(ti == 0)
    def _init():
        cu_scr[...] = cprev_ref[0]
        cs_scr[...] = sprev_ref[0]

    u = pc_ref[0] * ph_ref[0]
    c0 = cu_scr[0:1, :]
    c1 = cu_scr[1:2, :]
    u2 = _shift_rows(u, [c0, c1])
    u1 = _shift_rows(u, [c1])
    conv = u2 * cw_ref[0:1, :] + u1 * cw_ref[1:2, :] + u * cw_ref[2:3, :]
    yconv_ref[0] = pb_ref[0] * conv
    if tt >= 2:
        cu_scr[...] = u[tt - 2:tt, :]
    else:
        cu_scr[...] = jnp.concatenate([c1, u], axis=0)

    def shifted(p, lo, hi):
        prev = _shift_rows(p, [cs_scr[0:1, lo:hi]])
        q = p + (prev - p) * mu_ref[0:1, lo:hi]
        cs_scr[0:1, lo:hi] = p[tt - 1:tt, :]
        return q

    r = shifted(pr_ref[0], 0, rd)
    k = shifted(pk_ref[0], rd, 2 * rd)
    v = shifted(pv_ref[0], 2 * rd, 3 * rd)
    ql = shifted(pl_ref[0], 3 * rd, 3 * rd + LORA_BLOCK)

    ql_da = ql[:, 0:LANES]
    dec_in = w0_ref[...] + _dot(jnp.tanh(ql_da).astype(BF16), wdec_ref[...])
    z = -dec_in
    softplus = jnp.maximum(z, 0.0) + jnp.log(1.0 + jnp.exp(-jnp.abs(z)))
    w_log = -softplus - 0.5
    lw_ref[0] = -jnp.exp(w_log)
    a = jax.nn.sigmoid(a0_ref[...] + _dot(ql_da.astype(BF16), wa_ref[...]))
    g_ref[0] = _dot(jax.nn.sigmoid(ql[:, LANES:LANES + wg_ref.shape[0]]).astype(BF16), wg_ref[...])

    bd = bd_ref[...]
    kk = k * kk_ref[...]
    kk = kk / jnp.maximum(jnp.sqrt(_head_sum(kk * kk, bd)), L2_EPS)
    k = k * (1.0 + (a - 1.0) * ka_ref[...])
    r_ref[0] = r
    k_ref[0] = k
    v_ref[0] = v
    kn_ref[0] = kk
    b_ref[0] = kk * a
    bonus_ref[0] = _head_sum(r * k * rk_ref[...], bd) * v

    @pl.when(ti == pl.num_programs(1) - 1)
    def _fin():
        nconv_ref[0] = cu_scr[...]
        nshift_ref[0] = cs_scr[...]


def _prep(p3, conv_prev, shift_prev, wts):
    bsz, t, _ = p3.shape
    cc = conv_prev.shape[-1]
    rd = wts["w0"].shape[-1]
    sp = shift_prev.shape[-1]
    tt = _row_tile(t, TILES["prep_tt"])
    lora_blk = (3 * cc + 3 * rd) // LORA_BLOCK

    def col(width, idx):
        return pl.BlockSpec((1, tt, width), lambda b, i: (b, i, idx))

    def const(arr):
        return pl.BlockSpec(arr.shape, lambda b, i: (0,) * arr.ndim)

    names = ("conv_w", "mu", "w0", "wdec", "a0", "wa", "wg", "k_k", "k_a", "r_k", "bd")
    consts = [wts[n] for n in names]
    tok = jax.ShapeDtypeStruct((bsz, t, rd), F32)
    tok_spec = pl.BlockSpec((1, tt, rd), lambda b, i: (b, i, 0))
    return pl.pallas_call(
        _prep_body,
        out_shape=[jax.ShapeDtypeStruct((bsz, t, cc), F32)] + [tok] * 8
        + [jax.ShapeDtypeStruct((bsz, 2, cc), F32), jax.ShapeDtypeStruct((bsz, 1, sp), F32)],
        grid=(bsz, t // tt),
        in_specs=[col(cc, 0), col(cc, 1), col(cc, 2), col(rd, 3), col(rd, 4), col(rd, 5),
                  col(LORA_BLOCK, lora_blk),
                  pl.BlockSpec((1, 2, cc), lambda b, i: (b, 0, 0)),
                  pl.BlockSpec((1, 1, sp), lambda b, i: (b, 0, 0))]
        + [const(c) for c in consts],
        out_specs=[pl.BlockSpec((1, tt, cc), lambda b, i: (b, i, 0))] + [tok_spec] * 8
        + [pl.BlockSpec((1, 2, cc), lambda b, i: (b, 0, 0)), pl.BlockSpec((1, 1, sp), lambda b, i: (b, 0, 0))],
        scratch_shapes=[pltpu.VMEM((2, cc), F32), pltpu.VMEM((1, sp), F32)],
        compiler_params=_cparams(("parallel", "arbitrary")),
        name="prep",
    )(p3, p3, p3, p3, p3, p3, p3, conv_prev, shift_prev, *consts)


def _bd(y, mask):
    reps = mask.shape[0] // y.shape[0]
    return jnp.where(mask, jnp.concatenate([y] * reps, axis=0), jnp.zeros((), y.dtype))


def _mm_heads(x, y, mask, passes, nt=False):
    dot = _dot_nt if nt else _dot
    if passes == 1:
        return dot(x.astype(BF16), _bd(y.astype(BF16), mask))
    xh, xl = _split2(x)
    yh, yl = _split2(y)
    bh = _bd(yh, mask)
    return dot(xh, bh) + (dot(xh, _bd(yl, mask)) + dot(xl, bh))


def _wkv_body(r_ref, lw_ref, k_ref, v_ref, kn_ref, b_ref, s0_ref, y_ref, sout_ref, s_scr, *, nchunk):
    ti = pl.program_id(2)
    c = WKV_CHUNK
    n = HEAD_DIM
    wdt = WKV_GROUP * n
    hpb = WKV_GROUP
    ngroups = s_scr.shape[-1] // wdt

    @pl.when(ti == 0)
    def _init():
        s_scr[...] = s0_ref[0]

    row = lax.broadcasted_iota(jnp.int32, (c, c), 0)
    col = lax.broadcasted_iota(jnp.int32, (c, c), 1)
    ltri = jnp.where(row >= col, 1.0, 0.0).astype(BF16)
    prow = lax.broadcasted_iota(jnp.int32, (c, wdt), 0)
    pcol = lax.broadcasted_iota(jnp.int32, (c, wdt), 1) % n
    strict = prow > pcol
    incl = prow >= pcol
    brow = lax.broadcasted_iota(jnp.int32, (wdt, wdt), 0) // n
    bcol = lax.broadcasted_iota(jnp.int32, (wdt, wdt), 1) // n
    same_head = brow == bcol

    def group_chunk(sl, ln):
        lw = lw_ref[0, sl, ln]
        l_hi = lw.astype(BF16)
        rem = lw - l_hi.astype(F32)
        l_mid = rem.astype(BF16)
        l_lo = (rem - l_mid.astype(F32)).astype(BF16)
        cum = _dot(ltri, l_hi) + (_dot(ltri, l_mid) + _dot(ltri, l_lo))
        yield
        cum_last = cum[c - 1:c, :]
        g_end = jnp.exp(cum_last - cum)
        g_inv = jnp.exp(-cum)
        kk = k_ref[0, sl, ln]
        vv = v_ref[0, sl, ln]
        bb = b_ref[0, sl, ln]
        s0 = s_scr[:, ln]
        ap = jnp.concatenate([kn_ref[0, sl, ln] * jnp.exp(cum - lw), r_ref[0, sl, ln] * jnp.exp(cum)], axis=0)
        g_b = _mm_heads(ap, bb * g_inv, same_head, PASSES_GRAM, nt=True)
        g_k = _mm_heads(ap, kk * g_inv, same_head, PASSES_GRAM, nt=True)
        a_s = _mm_heads(ap, s0, same_head, PASSES_STATE, nt=True)
        yield
        m = jnp.where(strict, -g_b[:c], 0.0)
        lk = jnp.concatenate([jnp.where(strict, g_k[:c], 0.0), jnp.where(incl, g_k[c:], 0.0)], axis=0)
        t1 = a_s + _mm_heads(lk, vv, same_head, PASSES_GRAM)
        pm = _mm_heads(m, m, same_head, PASSES_SOLVE)
        yield
        w = t1[:c]
        w = w + _mm_heads(m, w, same_head, PASSES_SOLVE)
        yield
        for j in range(5):
            w = w + _mm_heads(pm, w, same_head, PASSES_SOLVE)
            if j < 4:
                pm = _mm_heads(pm, pm, same_head, PASSES_SOLVE)
            yield
        y_ref[0, sl, ln] = t1[c:] - _mm_heads(jnp.where(incl, g_b[c:], 0.0), w, same_head, PASSES_GRAM)
        vu = jnp.concatenate([vv, -w], axis=0)
        kb = jnp.concatenate([kk * g_end, bb * g_end], axis=0)
        if PASSES_STATE == 1:
            gram = _dot_tn(vu.astype(BF16), kb.astype(BF16))
        else:
            gram = _mm3(vu, kb, _dot_tn)
        gram = jnp.where(same_head, gram, 0.0)
        upd = gram[0:n]
        for h in range(1, hpb):
            upd = upd + gram[h * n:(h + 1) * n]
        s_scr[:, ln] = s0 * jnp.exp(cum_last) + upd

    def chunk(ci, carry):
        sl = pl.ds(pl.multiple_of(ci * c, c), c)
        live = [group_chunk(sl, slice(g * wdt, (g + 1) * wdt)) for g in range(ngroups)]
        while live:
            live = [gen for gen in live if next(gen, live) is not live]
        return carry

    lax.fori_loop(0, nchunk, chunk, 0)

    @pl.when(ti == pl.num_programs(2) - 1)
    def _fin():
        sout_ref[0] = s_scr[...]


def _wkv(r, lw, k, v, kn, b, s0):
    bsz, t, rd = r.shape
    tb = _row_tile(t, TILES["wkv_tb"])
    wdt = min(TILES["wkv_hpb"] * HEAD_DIM, rd)
    assert rd % wdt == 0 and wdt % (WKV_GROUP * HEAD_DIM) == 0
    tok = pl.BlockSpec((1, tb, wdt), lambda bi, hi, ti: (bi, ti, hi))
    st = pl.BlockSpec((1, HEAD_DIM, wdt), lambda bi, hi, ti: (bi, 0, hi))
    return pl.pallas_call(
        functools.partial(_wkv_body, nchunk=tb // WKV_CHUNK),
        out_shape=[jax.ShapeDtypeStruct((bsz, t, rd), F32),
                   jax.ShapeDtypeStruct((bsz, HEAD_DIM, rd), F32)],
        grid=(bsz, rd // wdt, t // tb),
        in_specs=[tok] * 6 + [st],
        out_specs=[tok, st],
        scratch_shapes=[pltpu.VMEM((HEAD_DIM, wdt), F32)],
        compiler_params=_cparams(("parallel", "parallel", "arbitrary")),
        name="wkv",
    )(r, lw, k, v, kn, b, s0)


def _mixout_body(h_ref, yc_ref, y_ref, bonus_ref, g_ref, lnw_ref, lnb_ref, bd_ref, wo_ref, gpost_ref, o_ref):
    cc = yc_ref.shape[-1]
    bd = bd_ref[...]
    y = y_ref[...]
    inv_n = 1.0 / HEAD_DIM
    mu = _head_sum(y, bd) * inv_n
    yc = y - mu
    var = _head_sum(yc * yc, bd) * inv_n
    yn = yc * lax.rsqrt(var + GN_EPS) * lnw_ref[...] + lnb_ref[...]
    y_rwkv = ((yn + bonus_ref[...]) * g_ref[...]).astype(BF16)
    m = _dot(yc_ref[...].astype(BF16), wo_ref[0:cc, :]) + _dot(y_rwkv, wo_ref[cc:, :])
    o_ref[...] = h_ref[...] + m * _rms_scale(m) * gpost_ref[...]


def _mixout(h, yc, y, bonus, g, lnw, lnb, bd, wo, g_post):
    m, d = h.shape
    cc = yc.shape[1]
    rd = y.shape[1]
    tm = _row_tile(m, TILES["mix_tm"])

    def rows(width):
        return pl.BlockSpec((tm, width), lambda i: (i, 0))

    def const(arr):
        return pl.BlockSpec(arr.shape, lambda i: (0,) * arr.ndim)

    return pl.pallas_call(
        _mixout_body,
        out_shape=jax.ShapeDtypeStruct((m, d), F32),
        grid=(m // tm,),
        in_specs=[rows(d), rows(cc), rows(rd), rows(rd), rows(rd),
                  const(lnw), const(lnb), const(bd), const(wo), const(g_post)],
        out_specs=rows(d),
        compiler_params=_cparams(("parallel",)),
        name="mixout",
    )(h, yc, y, bonus, g, lnw, lnb, bd, wo, g_post)


def _cast_pad(w, axis, extra):
    shape = list(w.shape)
    shape[axis] = extra
    return jnp.concatenate([w.astype(BF16), jnp.zeros(shape, BF16)], axis=axis)


def _prepare_weights(g_ffn1_pre, w_ffn1_gate, w_ffn1_up, w_ffn1_down, g_ffn1_post, g_mix_pre, w_in, conv_w,
                     tshift_mu, w0, w_decay_up, a0, w_a_up, w_g_up, k_k, k_a, r_k, ln_x_w, ln_x_b, w_out,
                     g_mix_post, g_ffn2_pre, w_ffn2_gate, w_ffn2_up, w_ffn2_down, g_ffn2_post):
    d_ff = w_ffn1_gate.shape[1]
    f_pad = _round_up(d_ff, 512) - d_ff
    rd = w0.shape[-1]
    dl, al, gl = w_decay_up.shape[0], w_a_up.shape[0], w_g_up.shape[0]
    assert dl + al == LANES and gl <= LORA_BLOCK - LANES and rd % HEAD_DIM == 0
    lora = dl + al + gl
    gl_pad = _round_up(gl, LANES)

    def row(vec):
        return vec.reshape(1, -1).astype(F32)

    def ffn_w(wg, wu, wd):
        return _cast_pad(wg, 1, f_pad), _cast_pad(wu, 1, f_pad), _cast_pad(wd, 0, f_pad)

    head = jnp.arange(rd, dtype=jnp.int32) // HEAD_DIM
    return dict(
        ffn1=(row(g_ffn1_pre),) + ffn_w(w_ffn1_gate, w_ffn1_up, w_ffn1_down) + (row(g_ffn1_post),),
        ffn2=(row(g_ffn2_pre),) + ffn_w(w_ffn2_gate, w_ffn2_up, w_ffn2_down) + (row(g_ffn2_post),),
        g_mix_pre=row(g_mix_pre),
        w_in=_cast_pad(w_in, 1, LORA_BLOCK - lora),
        lora=lora,
        conv_w=conv_w.astype(F32),
        mu=jnp.pad(row(tshift_mu), ((0, 0), (0, LORA_BLOCK - lora))),
        w0=row(w0),
        wdec=jnp.pad(w_decay_up.astype(BF16), ((0, al), (0, 0))),
        a0=row(a0),
        wa=jnp.pad(w_a_up.astype(BF16), ((dl, 0), (0, 0))),
        wg=jnp.pad(w_g_up.astype(BF16), ((0, gl_pad - gl), (0, 0))),
        k_k=row(k_k), k_a=row(k_a), r_k=row(r_k),
        bd=(head[:, None] == head[None, :]).astype(BF16),
        ln_w=row(ln_x_w), ln_b=row(ln_x_b),
        w_out=w_out.astype(BF16),
        g_mix_post=row(g_mix_post),
    )


def _layer(x3, conv_prev, shift_prev, wkv_prev, wts):
    bsz, t, d = x3.shape
    m = bsz * t
    lora = wts["lora"]
    x = x3.reshape(m, d)
    h = _ffn(x, *wts["ffn1"])
    p = _inproj(h, wts["g_mix_pre"], wts["w_in"])
    shift_pad = jnp.pad(shift_prev.astype(F32), ((0, 0), (0, 0), (0, LORA_BLOCK - lora)))
    yc, r, lw, k, v, kn, b, bonus, g, new_conv, new_shift = _prep(
        p.reshape(bsz, t, -1), conv_prev.astype(F32), shift_pad, wts)
    rd = r.shape[-1]
    nh, hd = wkv_prev.shape[1], wkv_prev.shape[2]
    s0 = wkv_prev.astype(F32).transpose(0, 2, 1, 3).reshape(bsz, hd, rd)
    y, s_new = _wkv(r, lw, k, v, kn, b, s0)
    new_wkv = s_new.reshape(bsz, hd, nh, hd).transpose(0, 2, 1, 3)
    h = _mixout(h, yc.reshape(m, -1), y.reshape(m, rd), bonus.reshape(m, rd), g.reshape(m, rd),
                wts["ln_w"], wts["ln_b"], wts["bd"], wts["w_out"], wts["g_mix_post"])
    out = _ffn(h, *wts["ffn2"])
    shift_dim = shift_prev.shape[-1]
    return out.reshape(bsz, t, d), new_conv, new_shift[:, :, :shift_dim], new_wkv


def kernel(x_prompt, x_sample, cache_conv, cache_shift, state_wkv, g_ffn1_pre, w_ffn1_gate, w_ffn1_up, w_ffn1_down, g_ffn1_post, g_mix_pre, w_in, conv_w, tshift_mu, w0, w_decay_up, a0, w_a_up, w_g_up, k_k, k_a, r_k, ln_x_w, ln_x_b, w_out, g_mix_post, g_ffn2_pre, w_ffn2_gate, w_ffn2_up, w_ffn2_down, g_ffn2_post):
    depth = w_in.shape[0]
    bp = x_prompt.shape[0]
    conv_rows, conv_ch = cache_conv.shape[2], cache_conv.shape[3]
    shift_dim = cache_shift.shape[-1]
    nh, hd = state_wkv.shape[2], state_wkv.shape[3]
    per_layer = (g_ffn1_pre, w_ffn1_gate, w_ffn1_up, w_ffn1_down, g_ffn1_post, g_mix_pre, w_in, conv_w,
                 tshift_mu, w0, w_decay_up, a0, w_a_up, w_g_up, k_k, k_a, r_k, ln_x_w, ln_x_b, w_out,
                 g_mix_post, g_ffn2_pre, w_ffn2_gate, w_ffn2_up, w_ffn2_down, g_ffn2_post)
    yp, ys = x_prompt, x_sample
    outs = [[] for _ in range(6)]
    for l in range(depth):
        wts = _prepare_weights(*(w[l] for w in per_layer))
        zc = jnp.zeros((bp, conv_rows, conv_ch), F32)
        zs = jnp.zeros((bp, 1, shift_dim), F32)
        zw = jnp.zeros((bp, nh, hd, hd), F32)
        yp, c1, s1, w1 = _layer(yp, zc, zs, zw, wts)
        ys, c2, s2, w2 = _layer(ys, cache_conv[l], cache_shift[l], state_wkv[l], wts)
        for lst, val, ref in zip(outs, (c1, s1, w1, c2, s2, w2),
                                 (cache_conv, cache_shift, state_wkv, cache_conv, cache_shift, state_wkv)):
            lst.append(val.astype(ref.dtype))
    return (yp, ys) + tuple(jnp.stack(o, 0) for o in outs)
```

```python
import functools

import jax
import jax.numpy as jnp
from jax import lax
from jax.experimental import pallas as pl
from jax.experimental.pallas import tpu as pltpu

F32 = jnp.float32
BF16 = jnp.bfloat16

RMS_EPS = 1e-6
GN_EPS = 64e-5
L2_EPS = 1e-12
HEAD_DIM = 64
WKV_CHUNK = 64
MXU_DIM = 256
WKV_GROUP = MXU_DIM // HEAD_DIM
LANES = 128
LORA_BLOCK = 512
VMEM_LIMIT = 56 * 1024 * 1024
TILES = dict(ffn_tm=512, ffn_tf=512, inproj_tm=1024, inproj_tn=1664, prep_tt=256, wkv_tb=512, wkv_hpb=16, mix_tm=256)
RESIDENT_ROWS = 1024
PASSES_GRAM = 1
PASSES_STATE = 1
PASSES_SOLVE = 1


def _cparams(sem):
    return pltpu.CompilerParams(dimension_semantics=sem, vmem_limit_bytes=VMEM_LIMIT)


def _round_up(n, m):
    return (n + m - 1) // m * m


def _row_tile(m, pref):
    return pref if m % pref == 0 else m


def _rows_plan(m, pref):
    if m <= RESIDENT_ROWS or m % pref:
        return m, pl.Buffered(1)
    return pref, None


def _rms_scale(x):
    return lax.rsqrt(jnp.mean(x * x, axis=-1, keepdims=True) + RMS_EPS)


def _dot(a, b):
    return jnp.dot(a, b, preferred_element_type=F32)


def _dot_nt(a, b):
    return lax.dot_general(a, b, (((1,), (1,)), ((), ())), preferred_element_type=F32)


def _dot_tn(a, b):
    return lax.dot_general(a, b, (((0,), (0,)), ((), ())), preferred_element_type=F32)


def _split2(x):
    hi = x.astype(BF16)
    lo = (x - hi.astype(F32)).astype(BF16)
    return hi, lo


def _mm3(a, b, dot=_dot):
    ah, al = _split2(a)
    bh, bl = _split2(b)
    return dot(ah, bh) + (dot(ah, bl) + dot(al, bh))


def _head_sum(x, bd):
    w = bd.shape[0]
    hi, lo = _split2(x)
    parts = [_dot(hi[:, s:s + w], bd) + _dot(lo[:, s:s + w], bd) for s in range(0, x.shape[1], w)]
    return jnp.concatenate(parts, axis=1)


def _ffn_body(x_ref, gpre_ref, wg_ref, wu_ref, wd_ref, *rest, nb_main, has_tail):
    if has_tail:
        wgt_ref, wut_ref, wdt_ref, gpost_ref, o_ref, xn_ref, acc_ref = rest
    else:
        gpost_ref, o_ref, xn_ref, acc_ref = rest
    j = pl.program_id(1)

    @pl.when(j == 0)
    def _init():
        x = x_ref[...]
        xn_ref[...] = (x * _rms_scale(x) * gpre_ref[...]).astype(BF16)
        acc_ref[...] = jnp.zeros_like(acc_ref)

    def accumulate(wg, wu, wd):
        xn = xn_ref[...]
        gate = _dot(xn, wg[...])
        up = _dot(xn, wu[...])
        hmid = (gate * jax.nn.sigmoid(gate) * up).astype(BF16)
        acc_ref[...] += _dot(hmid, wd[...])

    if has_tail:
        pl.when(j < nb_main)(lambda: accumulate(wg_ref, wu_ref, wd_ref))
        pl.when(j == nb_main)(lambda: accumulate(wgt_ref, wut_ref, wdt_ref))
    else:
        accumulate(wg_ref, wu_ref, wd_ref)

    @pl.when(j == pl.num_programs(1) - 1)
    def _fin():
        a = acc_ref[...]
        o_ref[...] = x_ref[...] + 0.5 * (a * _rms_scale(a) * gpost_ref[...])


def _ffn(x, g_pre, wg, wu, wd, wg_tail, wu_tail, wd_tail, g_post):
    m, d = x.shape
    f = wg.shape[1]
    tm, rows_mode = _rows_plan(m, TILES["ffn_tm"])
    tf = min(TILES["ffn_tf"], f)
    nb_main = f // tf
    f_tail = wg_tail.shape[1]
    assert nb_main * tf + f_tail == f and f_tail % LANES == 0
    has_tail = f_tail > 0
    last_main = nb_main - 1

    def whole(arr):
        return pl.BlockSpec(arr.shape, lambda i, j: (0, 0), pipeline_mode=pl.Buffered(1))

    tails = [wg_tail, wu_tail, wd_tail] if has_tail else []
    return pl.pallas_call(
        functools.partial(_ffn_body, nb_main=nb_main, has_tail=has_tail),
        out_shape=jax.ShapeDtypeStruct((m, d), F32),
        grid=(m // tm, nb_main + has_tail),
        in_specs=[
            pl.BlockSpec((tm, d), lambda i, j: (i, 0), pipeline_mode=rows_mode),
            pl.BlockSpec((1, d), lambda i, j: (0, 0)),
            pl.BlockSpec((d, tf), lambda i, j: (0, jnp.minimum(j, last_main))),
            pl.BlockSpec((d, tf), lambda i, j: (0, jnp.minimum(j, last_main))),
            pl.BlockSpec((tf, d), lambda i, j: (jnp.minimum(j, last_main), 0)),
        ] + [whole(t) for t in tails] + [pl.BlockSpec((1, d), lambda i, j: (0, 0))],
        out_specs=pl.BlockSpec((tm, d), lambda i, j: (i, 0), pipeline_mode=rows_mode),
        scratch_shapes=[pltpu.VMEM((tm, d), BF16), pltpu.VMEM((tm, d), F32)],
        compiler_params=_cparams(("parallel", "arbitrary")),
        name="ffn",
    )(x, g_pre, wg, wu, wd, *tails, g_post)


def _inproj_body(x_ref, g_ref, w_ref, o_ref, xn_ref):
    @pl.when(pl.program_id(1) == 0)
    def _init():
        x = x_ref[...]
        xn_ref[...] = (x * _rms_scale(x) * g_ref[...]).astype(BF16)

    o_ref[...] = _dot(xn_ref[...], w_ref[...])


def _inproj(x, g, w):
    m, d = x.shape
    n = w.shape[1]
    tm, rows_mode = _rows_plan(m, TILES["inproj_tm"])
    tn = _row_tile(n, TILES["inproj_tn"])
    return pl.pallas_call(
        _inproj_body,
        out_shape=jax.ShapeDtypeStruct((m, n), F32),
        grid=(m // tm, n // tn),
        in_specs=[
            pl.BlockSpec((tm, d), lambda i, j: (i, 0), pipeline_mode=rows_mode),
            pl.BlockSpec((1, d), lambda i, j: (0, 0)),
            pl.BlockSpec((d, tn), lambda i, j: (0, j)),
        ],
        out_specs=pl.BlockSpec((tm, tn), lambda i, j: (i, j)),
        scratch_shapes=[pltpu.VMEM((tm, d), BF16)],
        compiler_params=_cparams(("parallel", "arbitrary")),
        name="inproj",
    )(x, g, w)


def _shift_rows(x, carry_rows):
    n = len(carry_rows)
    out = pltpu.roll(x, n, axis=0)
    row = lax.broadcasted_iota(jnp.int32, x.shape, 0)
    for i, c in enumerate(carry_rows):
        out = jnp.where(row == i, c, out)
    return out


def _prep_body(pb_ref, pc_ref, ph_ref, pr_ref, pk_ref, pv_ref, pl_ref, cprev_ref, sprev_ref,
               cw_ref, mu_ref, w0_ref, wdec_ref, a0_ref, wa_ref, wg_ref, kk_ref, ka_ref, rk_ref, bd_ref,
               yconv_ref, r_ref, lw_ref, k_ref, v_ref, kn_ref, b_ref, bonus_ref, g_ref, nconv_ref, nshift_ref,
               cu_scr, cs_scr):
    ti = pl.program_id(1)
    cc = pc_ref.shape[-1]
    rd = pr_ref.shape[-1]
    tt = pc_ref.shape[1]

    @pl.when(ti == 0)
    def _init():
        cu_scr[...] = cprev_ref[0]
        cs_scr[...] = sprev_ref[0]

    u = pc_ref[0] * ph_ref[0]
    c0 = cu_scr[0:1, :]
    c1 = cu_scr[1:2, :]
    u2 = _shift_rows(u, [c0, c1])
    u1 = _shift_rows(u, [c1])
    conv = u2 * cw_ref[0:1, :] + u1 * cw_ref[1:2, :] + u * cw_ref[2:3, :]
    yconv_ref[0] = pb_ref[0] * conv
    if tt >= 2:
        cu_scr[...] = u[tt - 2:tt, :]
    else:
        cu_scr[...] = jnp.concatenate([c1, u], axis=0)

    def shifted(p, lo, hi):
        prev = _shift_rows(p, [cs_scr[0:1, lo:hi]])
        q = p + (prev - p) * mu_ref[0:1, lo:hi]
        cs_scr[0:1, lo:hi] = p[tt - 1:tt, :]
        return q

    r = shifted(pr_ref[0], 0, rd)
    k = shifted(pk_ref[0], rd, 2 * rd)
    v = shifted(pv_ref[0], 2 * rd, 3 * rd)
    ql = shifted(pl_ref[0], 3 * rd, 3 * rd + LORA_BLOCK)

    ql_da = ql[:, 0:LANES]
    dec_in = w0_ref[...] + _dot(jnp.tanh(ql_da).astype(BF16), wdec_ref[...])
    z = -dec_in
    softplus = jnp.maximum(z, 0.0) + jnp.log(1.0 + jnp.exp(-jnp.abs(z)))
    w_log = -softplus - 0.5
    lw_ref[0] = -jnp.exp(w_log)
    a = jax.nn.sigmoid(a0_ref[...] + _dot(ql_da.astype(BF16), wa_ref[...]))
    g_ref[0] = _dot(jax.nn.sigmoid(ql[:, LANES:LANES + wg_ref.shape[0]]).astype(BF16), wg_ref[...])

    bd = bd_ref[...]
    kk = k * kk_ref[...]
    kk = kk / jnp.maximum(jnp.sqrt(_head_sum(kk * kk, bd)), L2_EPS)
    k = k * (1.0 + (a - 1.0) * ka_ref[...])
    r_ref[0] = r
    k_ref[0] = k
    v_ref[0] = v
    kn_ref[0] = kk
    b_ref[0] = kk * a
    bonus_ref[0] = _head_sum(r * k * rk_ref[...], bd) * v

    @pl.when(ti == pl.num_programs(1) - 1)
    def _fin():
        nconv_ref[0] = cu_scr[...]
        nshift_ref[0] = cs_scr[...]


def _prep(p3, conv_prev, shift_prev, wts):
    bsz, t, _ = p3.shape
    cc = conv_prev.shape[-1]
    rd = wts["w0"].shape[-1]
    sp = shift_prev.shape[-1]
    tt = _row_tile(t, TILES["prep_tt"])
    lora_blk = (3 * cc + 3 * rd) // LORA_BLOCK

    def col(width, idx):
        return pl.BlockSpec((1, tt, width), lambda b, i: (b, i, idx))

    def const(arr):
        return pl.BlockSpec(arr.shape, lambda b, i: (0,) * arr.ndim)

    names = ("conv_w", "mu", "w0", "wdec", "a0", "wa", "wg", "k_k", "k_a", "r_k", "bd")
    consts = [wts[n] for n in names]
    tok = jax.ShapeDtypeStruct((bsz, t, rd), F32)
    tok_spec = pl.BlockSpec((1, tt, rd), lambda b, i: (b, i, 0))
    return pl.pallas_call(
        _prep_body,
        out_shape=[jax.ShapeDtypeStruct((bsz, t, cc), F32)] + [tok] * 8
        + [jax.ShapeDtypeStruct((bsz, 2, cc), F32), jax.ShapeDtypeStruct((bsz, 1, sp), F32)],
        grid=(bsz, t // tt),
        in_specs=[col(cc, 0), col(cc, 1), col(cc, 2), col(rd, 3), col(rd, 4), col(rd, 5),
                  col(LORA_BLOCK, lora_blk),
                  pl.BlockSpec((1, 2, cc), lambda b, i: (b, 0, 0)),
                  pl.BlockSpec((1, 1, sp), lambda b, i: (b, 0, 0))]
        + [const(c) for c in consts],
        out_specs=[pl.BlockSpec((1, tt, cc), lambda b, i: (b, i, 0))] + [tok_spec] * 8
        + [pl.BlockSpec((1, 2, cc), lambda b, i: (b, 0, 0)), pl.BlockSpec((1, 1, sp), lambda b, i: (b, 0, 0))],
        scratch_shapes=[pltpu.VMEM((2, cc), F32), pltpu.VMEM((1, sp), F32)],
        compiler_params=_cparams(("parallel", "arbitrary")),
        name="prep",
    )(p3, p3, p3, p3, p3, p3, p3, conv_prev, shift_prev, *consts)


def _bd(y, mask):
    reps = mask.shape[0] // y.shape[0]
    return jnp.where(mask, jnp.concatenate([y] * reps, axis=0), jnp.zeros((), y.dtype))


def _mm_heads(x, y, mask, passes, nt=False):
    dot = _dot_nt if nt else _dot
    if passes == 1:
        return dot(x.astype(BF16), _bd(y.astype(BF16), mask))
    xh, xl = _split2(x)
    yh, yl = _split2(y)
    bh = _bd(yh, mask)
    return dot(xh, bh) + (dot(xh, _bd(yl, mask)) + dot(xl, bh))


def _wkv_body(r_ref, lw_ref, k_ref, v_ref, kn_ref, b_ref, s0_ref, y_ref, sout_ref, s_scr, *, nchunk):
    ti = pl.program_id(2)
    c = WKV_CHUNK
    n = HEAD_DIM
    wdt = WKV_GROUP * n
    hpb = WKV_GROUP
    ngroups = s_scr.shape[-1] // wdt

    @pl.when(ti == 0)
    def _init():
        s_scr[...] = s0_ref[0]

    row = lax.broadcasted_iota(jnp.int32, (c, c), 0)
    col = lax.broadcasted_iota(jnp.int32, (c, c), 1)
    ltri = jnp.where(row >= col, 1.0, 0.0).astype(BF16)
    prow = lax.broadcasted_iota(jnp.int32, (c, wdt), 0)
    pcol = lax.broadcasted_iota(jnp.int32, (c, wdt), 1) % n
    strict = prow > pcol
    incl = prow >= pcol
    brow = lax.broadcasted_iota(jnp.int32, (wdt, wdt), 0) // n
    bcol = lax.broadcasted_iota(jnp.int32, (wdt, wdt), 1) // n
    same_head = brow == bcol

    def group_chunk(sl, ln):
        lw = lw_ref[0, sl, ln]
        l_hi = lw.astype(BF16)
        rem = lw - l_hi.astype(F32)
        l_mid = rem.astype(BF16)
        l_lo = (rem - l_mid.astype(F32)).astype(BF16)
        cum = _dot(ltri, l_hi) + (_dot(ltri, l_mid) + _dot(ltri, l_lo))
        yield
        cum_last = cum[c - 1:c, :]
        g_end = jnp.exp(cum_last - cum)
        g_inv = jnp.exp(-cum)
        kk = k_ref[0, sl, ln]
        vv = v_ref[0, sl, ln]
        bb = b_ref[0, sl, ln]
        s0 = s_scr[:, ln]
        ap = jnp.concatenate([kn_ref[0, sl, ln] * jnp.exp(cum - lw), r_ref[0, sl, ln] * jnp.exp(cum)], axis=0)
        g_b = _mm_heads(ap, bb * g_inv, same_head, PASSES_GRAM, nt=True)
        g_k = _mm_heads(ap, kk * g_inv, same_head, PASSES_GRAM, nt=True)
        a_s = _mm_heads(ap, s0, same_head, PASSES_STATE, nt=True)
        yield
        m = jnp.where(strict, -g_b[:c], 0.0)
        lk = jnp.concatenate([jnp.where(strict, g_k[:c], 0.0), jnp.where(incl, g_k[c:], 0.0)], axis=0)
        t1 = a_s + _mm_heads(lk, vv, same_head, PASSES_GRAM)
        pm = _mm_heads(m, m, same_head, PASSES_SOLVE)
        yield
        w = t1[:c]
        w = w + _mm_heads(m, w, same_head, PASSES_SOLVE)
        yield
        for j in range(5):
            w = w + _mm_heads(pm, w, same_head, PASSES_SOLVE)
            if j < 4:
                pm = _mm_heads(pm, pm, same_head, PASSES_SOLVE)
            yield
        y_ref[0, sl, ln] = t1[c:] - _mm_heads(jnp.where(incl, g_b[c:], 0.0), w, same_head, PASSES_GRAM)
        vu = jnp.concatenate([vv, -w], axis=0)
        kb = jnp.concatenate([kk * g_end, bb * g_end], axis=0)
        if PASSES_STATE == 1:
            gram = _dot_tn(vu.astype(BF16), kb.astype(BF16))
        else:
            gram = _mm3(vu, kb, _dot_tn)
        gram = jnp.where(same_head, gram, 0.0)
        upd = gram[0:n]
        for h in range(1, hpb):
            upd = upd + gram[h * n:(h + 1) * n]
        s_scr[:, ln] = s0 * jnp.exp(cum_last) + upd

    def chunk(ci, carry):
        sl = pl.ds(pl.multiple_of(ci * c, c), c)
        live = [group_chunk(sl, slice(g * wdt, (g + 1) * wdt)) for g in range(ngroups)]
        while live:
            live = [gen for gen in live if next(gen, live) is not live]
        return carry

    lax.fori_loop(0, nchunk, chunk, 0)

    @pl.when(ti == pl.num_programs(2) - 1)
    def _fin():
        sout_ref[0] = s_scr[...]


def _wkv(r, lw, k, v, kn, b, s0):
    bsz, t, rd = r.shape
    tb = _row_tile(t, TILES["wkv_tb"])
    wdt = min(TILES["wkv_hpb"] * HEAD_DIM, rd)
    assert rd % wdt == 0 and wdt % (WKV_GROUP * HEAD_DIM) == 0
    tok = pl.BlockSpec((1, tb, wdt), lambda bi, hi, ti: (bi, ti, hi))
    st = pl.BlockSpec((1, HEAD_DIM, wdt), lambda bi, hi, ti: (bi, 0, hi))
    return pl.pallas_call(
        functools.partial(_wkv_body, nchunk=tb // WKV_CHUNK),
        out_shape=[jax.ShapeDtypeStruct((bsz, t, rd), F32),
                   jax.ShapeDtypeStruct((bsz, HEAD_DIM, rd), F32)],
        grid=(bsz, rd // wdt, t // tb),
        in_specs=[tok] * 6 + [st],
        out_specs=[tok, st],
        scratch_shapes=[pltpu.VMEM((HEAD_DIM, wdt), F32)],
        compiler_params=_cparams(("parallel", "parallel", "arbitrary")),
        name="wkv",
    )(r, lw, k, v, kn, b, s0)


def _mixout_body(h_ref, yc_ref, y_ref, bonus_ref, g_ref, lnw_ref, lnb_ref, bd_ref, wo_ref, gpost_ref, o_ref):
    cc = yc_ref.shape[-1]
    bd = bd_ref[...]
    y = y_ref[...]
    inv_n = 1.0 / HEAD_DIM
    mu = _head_sum(y, bd) * inv_n
    yc = y - mu
    var = _head_sum(yc * yc, bd) * inv_n
    yn = yc * lax.rsqrt(var + GN_EPS) * lnw_ref[...] + lnb_ref[...]
    y_rwkv = ((yn + bonus_ref[...]) * g_ref[...]).astype(BF16)
    m = _dot(yc_ref[...].astype(BF16), wo_ref[0:cc, :]) + _dot(y_rwkv, wo_ref[cc:, :])
    o_ref[...] = h_ref[...] + m * _rms_scale(m) * gpost_ref[...]


def _mixout(h, yc, y, bonus, g, lnw, lnb, bd, wo, g_post):
    m, d = h.shape
    cc = yc.shape[1]
    rd = y.shape[1]
    tm = _row_tile(m, TILES["mix_tm"])

    def rows(width):
        return pl.BlockSpec((tm, width), lambda i: (i, 0))

    def const(arr):
        return pl.BlockSpec(arr.shape, lambda i: (0,) * arr.ndim)

    return pl.pallas_call(
        _mixout_body,
        out_shape=jax.ShapeDtypeStruct((m, d), F32),
        grid=(m // tm,),
        in_specs=[rows(d), rows(cc), rows(rd), rows(rd), rows(rd),
                  const(lnw), const(lnb), const(bd), const(wo), const(g_post)],
        out_specs=rows(d),
        compiler_params=_cparams(("parallel",)),
        name="mixout",
    )(h, yc, y, bonus, g, lnw, lnb, bd, wo, g_post)


def _cast_pad(w, axis, extra):
    shape = list(w.shape)
    shape[axis] = extra
    return jnp.concatenate([w.astype(BF16), jnp.zeros(shape, BF16)], axis=axis)


def _prepare_weights(g_ffn1_pre, w_ffn1_gate, w_ffn1_up, w_ffn1_down, g_ffn1_post, g_mix_pre, w_in, conv_w,
                     tshift_mu, w0, w_decay_up, a0, w_a_up, w_g_up, k_k, k_a, r_k, ln_x_w, ln_x_b, w_out,
                     g_mix_post, g_ffn2_pre, w_ffn2_gate, w_ffn2_up, w_ffn2_down, g_ffn2_post):
    d_ff = w_ffn1_gate.shape[1]
    rd = w0.shape[-1]
    dl, al, gl = w_decay_up.shape[0], w_a_up.shape[0], w_g_up.shape[0]
    assert dl + al == LANES and gl <= LORA_BLOCK - LANES and rd % HEAD_DIM == 0
    lora = dl + al + gl
    gl_pad = _round_up(gl, LANES)

    def row(vec):
        return vec.reshape(1, -1).astype(F32)

    f_main = d_ff // TILES["ffn_tf"] * TILES["ffn_tf"]

    def ffn_w(wg, wu, wd):
        return (wg.astype(BF16), wu.astype(BF16), wd.astype(BF16),
                wg[:, f_main:].astype(BF16), wu[:, f_main:].astype(BF16), wd[f_main:, :].astype(BF16))

    head = jnp.arange(MXU_DIM, dtype=jnp.int32) // HEAD_DIM
    return dict(
        ffn1=(row(g_ffn1_pre),) + ffn_w(w_ffn1_gate, w_ffn1_up, w_ffn1_down) + (row(g_ffn1_post),),
        ffn2=(row(g_ffn2_pre),) + ffn_w(w_ffn2_gate, w_ffn2_up, w_ffn2_down) + (row(g_ffn2_post),),
        g_mix_pre=row(g_mix_pre),
        w_in=_cast_pad(w_in, 1, LORA_BLOCK - lora),
        lora=lora,
        conv_w=conv_w.astype(F32),
        mu=jnp.pad(row(tshift_mu), ((0, 0), (0, LORA_BLOCK - lora))),
        w0=row(w0),
        wdec=jnp.pad(w_decay_up.astype(BF16), ((0, al), (0, 0))),
        a0=row(a0),
        wa=jnp.pad(w_a_up.astype(BF16), ((dl, 0), (0, 0))),
        wg=jnp.pad(w_g_up.astype(BF16), ((0, gl_pad - gl), (0, 0))),
        k_k=row(k_k), k_a=row(k_a), r_k=row(r_k),
        bd=(head[:, None] == head[None, :]).astype(BF16),
        ln_w=row(ln_x_w), ln_b=row(ln_x_b),
        w_out=w_out.astype(BF16),
        g_mix_post=row(g_mix_post),
    )


def _layer(x3, conv_prev, shift_prev, wkv_prev, wts):
    bsz, t, d = x3.shape
    m = bsz * t
    lora = wts["lora"]
    x = x3.reshape(m, d)
    h = _ffn(x, *wts["ffn1"])
    p = _inproj(h, wts["g_mix_pre"], wts["w_in"])
    shift_pad = jnp.pad(shift_prev.astype(F32), ((0, 0), (0, 0), (0, LORA_BLOCK - lora)))
    yc, r, lw, k, v, kn, b, bonus, g, new_conv, new_shift = _prep(
        p.reshape(bsz, t, -1), conv_prev.astype(F32), shift_pad, wts)
    rd = r.shape[-1]
    nh, hd = wkv_prev.shape[1], wkv_prev.shape[2]
    s0 = wkv_prev.astype(F32).transpose(0, 2, 1, 3).reshape(bsz, hd, rd)
    y, s_new = _wkv(r, lw, k, v, kn, b, s0)
    new_wkv = s_new.reshape(bsz, hd, nh, hd).transpose(0, 2, 1, 3)
    h = _mixout(h, yc.reshape(m, -1), y.reshape(m, rd), bonus.reshape(m, rd), g.reshape(m, rd),
                wts["ln_w"], wts["ln_b"], wts["bd"], wts["w_out"], wts["g_mix_post"])
    out = _ffn(h, *wts["ffn2"])
    shift_dim = shift_prev.shape[-1]
    return out.reshape(bsz, t, d), new_conv, new_shift[:, :, :shift_dim], new_wkv


def kernel(x_prompt, x_sample, cache_conv, cache_shift, state_wkv, g_ffn1_pre, w_ffn1_gate, w_ffn1_up, w_ffn1_down, g_ffn1_post, g_mix_pre, w_in, conv_w, tshift_mu, w0, w_decay_up, a0, w_a_up, w_g_up, k_k, k_a, r_k, ln_x_w, ln_x_b, w_out, g_mix_post, g_ffn2_pre, w_ffn2_gate, w_ffn2_up, w_ffn2_down, g_ffn2_post):
    depth = w_in.shape[0]
    bp = x_prompt.shape[0]
    conv_rows, conv_ch = cache_conv.shape[2], cache_conv.shape[3]
    shift_dim = cache_shift.shape[-1]
    nh, hd = state_wkv.shape[2], state_wkv.shape[3]
    per_layer = (g_ffn1_pre, w_ffn1_gate, w_ffn1_up, w_ffn1_down, g_ffn1_post, g_mix_pre, w_in, conv_w,
                 tshift_mu, w0, w_decay_up, a0, w_a_up, w_g_up, k_k, k_a, r_k, ln_x_w, ln_x_b, w_out,
                 g_mix_post, g_ffn2_pre, w_ffn2_gate, w_ffn2_up, w_ffn2_down, g_ffn2_post)
    yp, ys = x_prompt, x_sample
    outs = [[] for _ in range(6)]
    for l in range(depth):
        wts = _prepare_weights(*(w[l] for w in per_layer))
        zc = jnp.zeros((bp, conv_rows, conv_ch), F32)
        zs = jnp.zeros((bp, 1, shift_dim), F32)
        zw = jnp.zeros((bp, nh, hd, hd), F32)
        yp, c1, s1, w1 = _layer(yp, zc, zs, zw, wts)
        ys, c2, s2, w2 = _layer(ys, cache_conv[l], cache_shift[l], state_wkv[l], wts)
        for lst, val, ref in zip(outs, (c1, s1, w1, c2, s2, w2),
                                 (cache_conv, cache_shift, state_wkv, cache_conv, cache_shift, state_wkv)):
            lst.append(val.astype(ref.dtype))
    return (yp, ys) + tuple(jnp.stack(o, 0) for o in outs)
```

```python
import functools

import jax
import jax.numpy as jnp
from jax import lax
from jax.experimental import pallas as pl
from jax.experimental.pallas import tpu as pltpu

F32 = jnp.float32
BF16 = jnp.bfloat16

RMS_EPS = 1e-6
GN_EPS = 64e-5
L2_EPS = 1e-12
HEAD_DIM = 64
WKV_CHUNK = 64
MXU_DIM = 256
WKV_GROUP = MXU_DIM // HEAD_DIM
PIPE_CHUNKS = 2
LANES = 128
LORA_BLOCK = 512
VMEM_LIMIT = 56 * 1024 * 1024
TILES = dict(ffn_tm=512, ffn_tf=512, inproj_tm=1024, inproj_tn=1664, mixer_tt=256, outproj_tm=512)
RESIDENT_ROWS = 1024


def _cparams(sem):
    return pltpu.CompilerParams(dimension_semantics=sem, vmem_limit_bytes=VMEM_LIMIT)


def _round_up(n, m):
    return (n + m - 1) // m * m


def _row_tile(m, pref):
    return pref if m % pref == 0 else m


def _rows_plan(m, pref):
    if m <= RESIDENT_ROWS or m % pref:
        return m, pl.Buffered(1)
    return pref, None


def _rms_scale(x):
    return lax.rsqrt(jnp.mean(x * x, axis=-1, keepdims=True) + RMS_EPS)


def _dot(a, b):
    return jnp.dot(a, b, preferred_element_type=F32)


def _dot_nt(a, b):
    return lax.dot_general(a, b, (((1,), (1,)), ((), ())), preferred_element_type=F32)


def _dot_tn(a, b):
    return lax.dot_general(a, b, (((0,), (0,)), ((), ())), preferred_element_type=F32)


def _split2(x):
    hi = x.astype(BF16)
    lo = (x - hi.astype(F32)).astype(BF16)
    return hi, lo


def _head_sum(x, bd):
    w = bd.shape[0]
    rows = x.shape[0]
    hi, lo = _split2(x)
    starts = range(0, x.shape[1], w)
    stacked = jnp.concatenate([part[:, s:s + w] for s in starts for part in (hi, lo)], axis=0)
    sums = _dot(stacked, bd)
    parts = [sums[2 * i * rows:(2 * i + 1) * rows] + sums[(2 * i + 1) * rows:(2 * i + 2) * rows]
             for i in range(len(starts))]
    return jnp.concatenate(parts, axis=1)


def _ffn_body(x_ref, gpre_ref, wg_ref, wu_ref, wd_ref, *rest, nb_main, has_tail):
    if has_tail:
        wgt_ref, wut_ref, wdt_ref, gpost_ref, o_ref, xn_ref, acc_ref = rest
    else:
        gpost_ref, o_ref, xn_ref, acc_ref = rest
    j = pl.program_id(1)

    @pl.when(j == 0)
    def _init():
        x = x_ref[...]
        xn_ref[...] = (x * _rms_scale(x) * gpre_ref[...]).astype(BF16)
        acc_ref[...] = jnp.zeros_like(acc_ref)

    def accumulate(wg, wu, wd):
        xn = xn_ref[...]
        gate = _dot(xn, wg[...])
        up = _dot(xn, wu[...])
        hmid = (gate * jax.nn.sigmoid(gate) * up).astype(BF16)
        acc_ref[...] += _dot(hmid, wd[...])

    if has_tail:
        pl.when(j < nb_main)(lambda: accumulate(wg_ref, wu_ref, wd_ref))
        pl.when(j == nb_main)(lambda: accumulate(wgt_ref, wut_ref, wdt_ref))
    else:
        accumulate(wg_ref, wu_ref, wd_ref)

    @pl.when(j == pl.num_programs(1) - 1)
    def _fin():
        a = acc_ref[...]
        o_ref[...] = x_ref[...] + 0.5 * (a * _rms_scale(a) * gpost_ref[...])


def _ffn(x, g_pre, wg, wu, wd, wg_tail, wu_tail, wd_tail, g_post):
    m, d = x.shape
    f = wg.shape[1]
    tm, rows_mode = _rows_plan(m, TILES["ffn_tm"])
    tf = min(TILES["ffn_tf"], f)
    nb_main = f // tf
    f_tail = wg_tail.shape[1]
    assert nb_main * tf + f_tail == f and f_tail % LANES == 0
    has_tail = f_tail > 0
    last_main = nb_main - 1

    def whole(arr):
        return pl.BlockSpec(arr.shape, lambda i, j: (0, 0), pipeline_mode=pl.Buffered(1))

    tails = [wg_tail, wu_tail, wd_tail] if has_tail else []
    return pl.pallas_call(
        functools.partial(_ffn_body, nb_main=nb_main, has_tail=has_tail),
        out_shape=jax.ShapeDtypeStruct((m, d), F32),
        grid=(m // tm, nb_main + has_tail),
        in_specs=[
            pl.BlockSpec((tm, d), lambda i, j: (i, 0), pipeline_mode=rows_mode),
            pl.BlockSpec((1, d), lambda i, j: (0, 0)),
            pl.BlockSpec((d, tf), lambda i, j: (0, jnp.minimum(j, last_main))),
            pl.BlockSpec((d, tf), lambda i, j: (0, jnp.minimum(j, last_main))),
            pl.BlockSpec((tf, d), lambda i, j: (jnp.minimum(j, last_main), 0)),
        ] + [whole(t) for t in tails] + [pl.BlockSpec((1, d), lambda i, j: (0, 0))],
        out_specs=pl.BlockSpec((tm, d), lambda i, j: (i, 0), pipeline_mode=rows_mode),
        scratch_shapes=[pltpu.VMEM((tm, d), BF16), pltpu.VMEM((tm, d), F32)],
        compiler_params=_cparams(("parallel", "arbitrary")),
        name="ffn",
    )(x, g_pre, wg, wu, wd, *tails, g_post)


def _inproj_body(x_ref, g_ref, w_ref, o_ref, xn_ref):
    @pl.when(pl.program_id(1) == 0)
    def _init():
        x = x_ref[...]
        xn_ref[...] = (x * _rms_scale(x) * g_ref[...]).astype(BF16)

    o_ref[...] = _dot(xn_ref[...], w_ref[...])


def _inproj(x, g, w):
    m, d = x.shape
    n = w.shape[1]
    tm, rows_mode = _rows_plan(m, TILES["inproj_tm"])
    tn = _row_tile(n, TILES["inproj_tn"])
    return pl.pallas_call(
        _inproj_body,
        out_shape=jax.ShapeDtypeStruct((m, n), F32),
        grid=(m // tm, n // tn),
        in_specs=[
            pl.BlockSpec((tm, d), lambda i, j: (i, 0), pipeline_mode=rows_mode),
            pl.BlockSpec((1, d), lambda i, j: (0, 0)),
            pl.BlockSpec((d, tn), lambda i, j: (0, j)),
        ],
        out_specs=pl.BlockSpec((tm, tn), lambda i, j: (i, j)),
        scratch_shapes=[pltpu.VMEM((tm, d), BF16)],
        compiler_params=_cparams(("parallel", "arbitrary")),
        name="inproj",
    )(x, g, w)


def _shift_rows(x, carry_rows):
    n = len(carry_rows)
    out = pltpu.roll(x, n, axis=0)
    row = lax.broadcasted_iota(jnp.int32, x.shape, 0)
    for i, c in enumerate(carry_rows):
        out = jnp.where(row == i, c, out)
    return out


def _bd(y, mask):
    reps = mask.shape[0] // y.shape[0]
    return jnp.where(mask, jnp.concatenate([y] * reps, axis=0), jnp.zeros((), y.dtype))


def _mm_heads(x, y, mask, nt=False):
    dot = _dot_nt if nt else _dot
    return dot(x.astype(BF16), _bd(y.astype(BF16), mask))


def _round_robin(tasks):
    while tasks:
        tasks = [t for t in tasks if next(t, tasks) is not tasks]


def _mixer_body(pb_ref, pc_ref, ph_ref, pr_ref, pk_ref, pv_ref, pl_ref, cprev_ref, sprev_ref, s0_ref,
                cw_ref, mu_ref, w0_ref, wdec_ref, a0_ref, wa_ref, wg_ref, kk_ref, ka_ref, rk_ref,
                lnw_ref, lnb_ref, bd_ref,
                mixc_ref, mixr_ref, nconv_ref, nshift_ref, sout_ref,
                cu_scr, cs_scr, s_scr, r_scr, lw_scr, k_scr, v_scr, kn_scr, b_scr, y_scr, bonus_scr, g_scr):
    ti = pl.program_id(1)
    c = WKV_CHUNK
    n = HEAD_DIM
    rd = pr_ref.shape[-1]
    nchunk = pc_ref.shape[1] // c
    wdt = WKV_GROUP * n
    ngroups = rd // wdt

    @pl.when(ti == 0)
    def _init():
        cu_scr[...] = cprev_ref[0]
        cs_scr[...] = sprev_ref[0]
        s_scr[...] = s0_ref[0]

    bd = bd_ref[...]
    row = lax.broadcasted_iota(jnp.int32, (c, c), 0)
    col = lax.broadcasted_iota(jnp.int32, (c, c), 1)
    ltri = jnp.where(row >= col, 1.0, 0.0).astype(BF16)
    prow = lax.broadcasted_iota(jnp.int32, (c, wdt), 0)
    pcol = lax.broadcasted_iota(jnp.int32, (c, wdt), 1) % n
    strict = prow > pcol
    incl = prow >= pcol
    brow = lax.broadcasted_iota(jnp.int32, (wdt, wdt), 0) // n
    bcol = lax.broadcasted_iota(jnp.int32, (wdt, wdt), 1) // n
    same_head = brow == bcol

    def prep_chunk(sl):
        u = pc_ref[0, sl, :] * ph_ref[0, sl, :]
        c0 = cu_scr[0:1, :]
        c1 = cu_scr[1:2, :]
        u2 = _shift_rows(u, [c0, c1])
        u1 = _shift_rows(u, [c1])
        conv = u2 * cw_ref[0:1, :] + u1 * cw_ref[1:2, :] + u * cw_ref[2:3, :]
        mixc_ref[0, sl, :] = (pb_ref[0, sl, :] * conv).astype(mixc_ref.dtype)
        cu_scr[...] = u[c - 2:c, :]
        yield

        def shifted(p, lo, hi):
            prev = _shift_rows(p, [cs_scr[0:1, lo:hi]])
            q = p + (prev - p) * mu_ref[0:1, lo:hi]
            cs_scr[0:1, lo:hi] = p[c - 1:c, :]
            return q

        ql = shifted(pl_ref[0, sl, :], 3 * rd, 3 * rd + LORA_BLOCK)
        ql_da = ql[:, 0:LANES]
        dec_in = w0_ref[...] + _dot(jnp.tanh(ql_da).astype(BF16), wdec_ref[...])
        a = jax.nn.sigmoid(a0_ref[...] + _dot(ql_da.astype(BF16), wa_ref[...]))
        g_scr[sl, :] = _dot(jax.nn.sigmoid(ql[:, LANES:LANES + wg_ref.shape[0]]).astype(BF16), wg_ref[...])
        yield
        z = -dec_in
        softplus = jnp.maximum(z, 0.0) + jnp.log(1.0 + jnp.exp(-jnp.abs(z)))
        w_log = -softplus - 0.5
        lw_scr[sl, :] = -jnp.exp(w_log)
        k = shifted(pk_ref[0, sl, :], rd, 2 * rd)
        kk = k * kk_ref[...]
        ssq = _head_sum(kk * kk, bd)
        yield
        kk = kk / jnp.maximum(jnp.sqrt(ssq), L2_EPS)
        kn_scr[sl, :] = kk
        b_scr[sl, :] = kk * a
        k = k * (1.0 + (a - 1.0) * ka_ref[...])
        k_scr[sl, :] = k
        r = shifted(pr_ref[0, sl, :], 0, rd)
        r_scr[sl, :] = r
        rk_sum = _head_sum(r * k * rk_ref[...], bd)
        yield
        v = shifted(pv_ref[0, sl, :], 2 * rd, 3 * rd)
        v_scr[sl, :] = v
        bonus_scr[sl, :] = rk_sum * v

    def wkv_pre(sl, ln, out):
        lw = lw_scr[sl, ln]
        l_hi = lw.astype(BF16)
        rem = lw - l_hi.astype(F32)
        l_mid = rem.astype(BF16)
        l_lo = (rem - l_mid.astype(F32)).astype(BF16)
        cum = _dot(ltri, l_hi) + (_dot(ltri, l_mid) + _dot(ltri, l_lo))
        yield
        cum_last = cum[c - 1:c, :]
        g_end = jnp.exp(cum_last - cum)
        g_inv = jnp.exp(-cum)
        kk = k_scr[sl, ln]
        vv = v_scr[sl, ln]
        bb = b_scr[sl, ln]
        ap = jnp.concatenate([kn_scr[sl, ln] * jnp.exp(cum - lw), r_scr[sl, ln] * jnp.exp(cum)], axis=0)
        g_b = _mm_heads(ap, bb * g_inv, same_head, nt=True)
        g_k = _mm_heads(ap, kk * g_inv, same_head, nt=True)
        yield
        q = jnp.where(strict, -g_b[:c], 0.0)
        lk = jnp.concatenate([jnp.where(strict, g_k[:c], 0.0), jnp.where(incl, g_k[c:], 0.0)], axis=0)
        t1v = _mm_heads(lk, vv, same_head)
        z = q
        for _ in range(5):
            qb = _bd(q.astype(BF16), same_head)
            zq_qq = _dot(jnp.concatenate([z, q], axis=0).astype(BF16), qb)
            yield
            z = z + _dot((q + zq_qq[:c]).astype(BF16), qb)
            q = zq_qq[c:]
            yield
        out.update(ap=ap.astype(BF16), t1v=t1v, z=z.astype(BF16), irb=jnp.where(incl, g_b[c:], 0.0).astype(BF16),
                   vv=vv, kb=jnp.concatenate([kk * g_end, bb * g_end], axis=0).astype(BF16),
                   g_last=jnp.exp(cum_last))

    def wkv_post(sl, ln, pre):
        s0 = s_scr[:, ln]
        a_s = _dot_nt(pre["ap"], _bd(s0.astype(BF16), same_head))
        yield
        t1 = a_s + pre["t1v"]
        u = t1[:c] + _dot(pre["z"], _bd(t1[:c].astype(BF16), same_head))
        yield
        ub = u.astype(BF16)
        y_scr[sl, ln] = t1[c:] - _dot(pre["irb"], _bd(ub, same_head))
        vu = jnp.concatenate([pre["vv"].astype(BF16), -ub], axis=0)
        gram = jnp.where(same_head, _dot_tn(vu, pre["kb"]), 0.0)
        upd = gram[0:n]
        for h in range(1, WKV_GROUP):
            upd = upd + gram[h * n:(h + 1) * n]
        s_scr[:, ln] = s0 * pre["g_last"] + upd

    def out_chunk(sl):
        y = y_scr[sl, :]
        inv_n = 1.0 / HEAD_DIM
        mean = _head_sum(y, bd) * inv_n
        yield
        yc = y - mean
        var = _head_sum(yc * yc, bd) * inv_n
        yield
        yn = yc * lax.rsqrt(var + GN_EPS) * lnw_ref[...] + lnb_ref[...]
        mixr_ref[0, sl, :] = ((yn + bonus_scr[sl, :]) * g_scr[sl, :]).astype(mixr_ref.dtype)

    def rows(i):
        return slice(i * c, (i + 1) * c)

    def in_turn(gens):
        for gen in gens:
            yield from gen

    lanes = [slice(g * wdt, (g + 1) * wdt) for g in range(ngroups)]
    pre = [[{} for _ in lanes] for _ in range(nchunk)]
    span = PIPE_CHUNKS if nchunk % PIPE_CHUNKS == 0 else 1
    nstage = nchunk // span

    def chunks_of(stage):
        return range(stage * span, (stage + 1) * span) if 0 <= stage < nstage else ()

    for step in range(nstage + 3):
        tasks = [in_turn([wkv_post(rows(i), ln, pre[i][g]) for i in chunks_of(step - 2)])
                 for g, ln in enumerate(lanes)]
        tasks += [wkv_pre(rows(i), ln, pre[i][g]) for i in chunks_of(step - 1) for g, ln in enumerate(lanes)]
        tasks.append(in_turn([prep_chunk(rows(i)) for i in chunks_of(step)]))
        tasks.append(in_turn([out_chunk(rows(i)) for i in chunks_of(step - 3)]))
        _round_robin(tasks)

    @pl.when(ti == pl.num_programs(1) - 1)
    def _fin():
        nconv_ref[0] = cu_scr[...]
        nshift_ref[0] = cs_scr[...]
        sout_ref[0] = s_scr[...]


def _mixer(p3, conv_prev, shift_prev, s0, wts):
    bsz, t, _ = p3.shape
    cc = conv_prev.shape[-1]
    rd = wts["w0"].shape[-1]
    sp = shift_prev.shape[-1]
    tt = _row_tile(t, TILES["mixer_tt"])
    assert tt % WKV_CHUNK == 0 and rd % (WKV_GROUP * HEAD_DIM) == 0
    lora_blk = (3 * cc + 3 * rd) // LORA_BLOCK

    def col(width, idx):
        return pl.BlockSpec((1, tt, width), lambda b, i: (b, i, idx))

    def const(arr):
        return pl.BlockSpec(arr.shape, lambda b, i: (0,) * arr.ndim)

    def per_stream(rows, width):
        return pl.BlockSpec((1, rows, width), lambda b, i: (b, 0, 0))

    names = ("conv_w", "mu", "w0", "wdec", "a0", "wa", "wg", "k_k", "k_a", "r_k", "ln_w", "ln_b", "bd")
    consts = [wts[n] for n in names]
    tok = pltpu.VMEM((tt, rd), F32)
    return pl.pallas_call(
        _mixer_body,
        out_shape=[jax.ShapeDtypeStruct((bsz, t, cc), BF16), jax.ShapeDtypeStruct((bsz, t, rd), BF16),
                   jax.ShapeDtypeStruct((bsz, 2, cc), F32), jax.ShapeDtypeStruct((bsz, 1, sp), F32),
                   jax.ShapeDtypeStruct((bsz, HEAD_DIM, rd), F32)],
        grid=(bsz, t // tt),
        in_specs=[col(cc, 0), col(cc, 1), col(cc, 2), col(rd, 3), col(rd, 4), col(rd, 5),
                  col(LORA_BLOCK, lora_blk),
                  per_stream(2, cc), per_stream(1, sp), per_stream(HEAD_DIM, rd)]
        + [const(c) for c in consts],
        out_specs=[pl.BlockSpec((1, tt, cc), lambda b, i: (b, i, 0)), pl.BlockSpec((1, tt, rd), lambda b, i: (b, i, 0)),
                   per_stream(2, cc), per_stream(1, sp), per_stream(HEAD_DIM, rd)],
        scratch_shapes=[pltpu.VMEM((2, cc), F32), pltpu.VMEM((1, sp), F32), pltpu.VMEM((HEAD_DIM, rd), F32)]
        + [tok] * 9,
        compiler_params=_cparams(("parallel", "arbitrary")),
        name="mixer",
    )(p3, p3, p3, p3, p3, p3, p3, conv_prev, shift_prev, s0, *consts)


def _outproj_body(h_ref, mc_ref, mr_ref, wo_ref, gpost_ref, o_ref):
    cc = mc_ref.shape[-1]
    m = _dot(mc_ref[...], wo_ref[0:cc, :]) + _dot(mr_ref[...], wo_ref[cc:, :])
    o_ref[...] = h_ref[...] + m * _rms_scale(m) * gpost_ref[...]


def _outproj(h, mix_c, mix_r, wo, g_post):
    m, d = h.shape
    tm = _row_tile(m, TILES["outproj_tm"])

    def rows(width):
        return pl.BlockSpec((tm, width), lambda i: (i, 0))

    return pl.pallas_call(
        _outproj_body,
        out_shape=jax.ShapeDtypeStruct((m, d), F32),
        grid=(m // tm,),
        in_specs=[rows(d), rows(mix_c.shape[1]), rows(mix_r.shape[1]),
                  pl.BlockSpec(wo.shape, lambda i: (0, 0), pipeline_mode=pl.Buffered(1)),
                  pl.BlockSpec(g_post.shape, lambda i: (0, 0))],
        out_specs=rows(d),
        compiler_params=_cparams(("parallel",)),
        name="outproj",
    )(h, mix_c, mix_r, wo, g_post)


def _cast_pad(w, axis, extra):
    shape = list(w.shape)
    shape[axis] = extra
    return jnp.concatenate([w.astype(BF16), jnp.zeros(shape, BF16)], axis=axis)


def _prepare_weights(g_ffn1_pre, w_ffn1_gate, w_ffn1_up, w_ffn1_down, g_ffn1_post, g_mix_pre, w_in, conv_w,
                     tshift_mu, w0, w_decay_up, a0, w_a_up, w_g_up, k_k, k_a, r_k, ln_x_w, ln_x_b, w_out,
                     g_mix_post, g_ffn2_pre, w_ffn2_gate, w_ffn2_up, w_ffn2_down, g_ffn2_post):
    d_ff = w_ffn1_gate.shape[1]
    rd = w0.shape[-1]
    dl, al, gl = w_decay_up.shape[0], w_a_up.shape[0], w_g_up.shape[0]
    assert dl + al == LANES and gl <= LORA_BLOCK - LANES and rd % HEAD_DIM == 0
    lora = dl + al + gl
    gl_pad = _round_up(gl, LANES)

    def row(vec):
        return vec.reshape(1, -1).astype(F32)

    f_main = d_ff // TILES["ffn_tf"] * TILES["ffn_tf"]

    def ffn_w(wg, wu, wd):
        return (wg.astype(BF16), wu.astype(BF16), wd.astype(BF16),
                wg[:, f_main:].astype(BF16), wu[:, f_main:].astype(BF16), wd[f_main:, :].astype(BF16))

    head = jnp.arange(MXU_DIM, dtype=jnp.int32) // HEAD_DIM
    return dict(
        ffn1=(row(g_ffn1_pre),) + ffn_w(w_ffn1_gate, w_ffn1_up, w_ffn1_down) + (row(g_ffn1_post),),
        ffn2=(row(g_ffn2_pre),) + ffn_w(w_ffn2_gate, w_ffn2_up, w_ffn2_down) + (row(g_ffn2_post),),
        g_mix_pre=row(g_mix_pre),
        w_in=_cast_pad(w_in, 1, LORA_BLOCK - lora),
        lora=lora,
        conv_w=conv_w.astype(F32),
        mu=jnp.pad(row(tshift_mu), ((0, 0), (0, LORA_BLOCK - lora))),
        w0=row(w0),
        wdec=jnp.pad(w_decay_up.astype(BF16), ((0, al), (0, 0))),
        a0=row(a0),
        wa=jnp.pad(w_a_up.astype(BF16), ((dl, 0), (0, 0))),
        wg=jnp.pad(w_g_up.astype(BF16), ((0, gl_pad - gl), (0, 0))),
        k_k=row(k_k), k_a=row(k_a), r_k=row(r_k),
        bd=(head[:, None] == head[None, :]).astype(BF16),
        ln_w=row(ln_x_w), ln_b=row(ln_x_b),
        w_out=w_out.astype(BF16),
        g_mix_post=row(g_mix_post),
    )


def _layer(x3, conv_prev, shift_prev, wkv_prev, wts):
    bsz, t, d = x3.shape
    m = bsz * t
    lora = wts["lora"]
    x = x3.reshape(m, d)
    h = _ffn(x, *wts["ffn1"])
    p = _inproj(h, wts["g_mix_pre"], wts["w_in"])
    shift_pad = jnp.pad(shift_prev.astype(F32), ((0, 0), (0, 0), (0, LORA_BLOCK - lora)))
    nh, hd = wkv_prev.shape[1], wkv_prev.shape[2]
    s0 = wkv_prev.astype(F32).transpose(0, 2, 1, 3).reshape(bsz, hd, nh * hd)
    mix_c, mix_r, new_conv, new_shift, s_new = _mixer(
        p.reshape(bsz, t, -1), conv_prev.astype(F32), shift_pad, s0, wts)
    new_wkv = s_new.reshape(bsz, hd, nh, hd).transpose(0, 2, 1, 3)
    h = _outproj(h, mix_c.reshape(m, -1), mix_r.reshape(m, -1), wts["w_out"], wts["g_mix_post"])
    out = _ffn(h, *wts["ffn2"])
    shift_dim = shift_prev.shape[-1]
    return out.reshape(bsz, t, d), new_conv, new_shift[:, :, :shift_dim], new_wkv


def kernel(x_prompt, x_sample, cache_conv, cache_shift, state_wkv, g_ffn1_pre, w_ffn1_gate, w_ffn1_up, w_ffn1_down, g_ffn1_post, g_mix_pre, w_in, conv_w, tshift_mu, w0, w_decay_up, a0, w_a_up, w_g_up, k_k, k_a, r_k, ln_x_w, ln_x_b, w_out, g_mix_post, g_ffn2_pre, w_ffn2_gate, w_ffn2_up, w_ffn2_down, g_ffn2_post):
    depth = w_in.shape[0]
    bp = x_prompt.shape[0]
    conv_rows, conv_ch = cache_conv.shape[2], cache_conv.shape[3]
    shift_dim = cache_shift.shape[-1]
    nh, hd = state_wkv.shape[2], state_wkv.shape[3]
    per_layer = (g_ffn1_pre, w_ffn1_gate, w_ffn1_up, w_ffn1_down, g_ffn1_post, g_mix_pre, w_in, conv_w,
                 tshift_mu, w0, w_decay_up, a0, w_a_up, w_g_up, k_k, k_a, r_k, ln_x_w, ln_x_b, w_out,
                 g_mix_post, g_ffn2_pre, w_ffn2_gate, w_ffn2_up, w_ffn2_down, g_ffn2_post)
    yp, ys = x_prompt, x_sample
    outs = [[] for _ in range(6)]
    for l in range(depth):
        wts = _prepare_weights(*(w[l] for w in per_layer))
        zc = jnp.zeros((bp, conv_rows, conv_ch), F32)
        zs = jnp.zeros((bp, 1, shift_dim), F32)
        zw = jnp.zeros((bp, nh, hd, hd), F32)
        yp, c1, s1, w1 = _layer(yp, zc, zs, zw, wts)
        ys, c2, s2, w2 = _layer(ys, cache_conv[l], cache_shift[l], state_wkv[l], wts)
        for lst, val, ref in zip(outs, (c1, s1, w1, c2, s2, w2),
                                 (cache_conv, cache_shift, state_wkv, cache_conv, cache_shift, state_wkv)):
            lst.append(val.astype(ref.dtype))
    return (yp, ys) + tuple(jnp.stack(o, 0) for o in outs)
```

```python
import functools

import jax
import jax.numpy as jnp
from jax import lax
from jax.experimental import pallas as pl
from jax.experimental.pallas import tpu as pltpu

F32 = jnp.float32
BF16 = jnp.bfloat16

RMS_EPS = 1e-6
GN_EPS = 64e-5
L2_EPS = 1e-12
HEAD_DIM = 64
WKV_CHUNK = 64
MXU_DIM = 256
WKV_GROUP = MXU_DIM // HEAD_DIM
PIPE_CHUNKS = 2
LANES = 128
LORA_BLOCK = 512
VMEM_LIMIT = 56 * 1024 * 1024
VMEM_BUDGET = VMEM_LIMIT - 4 * 1024 * 1024
TILES = dict(ffn_tm=512, ffn_tf=(1024, 512), inproj_tm=1024, inproj_tn=1664, mixer_tt=256, outproj_tm=512)
RESIDENT_ROWS = 1024


def _cparams(sem):
    return pltpu.CompilerParams(dimension_semantics=sem, vmem_limit_bytes=VMEM_LIMIT)


def _round_up(n, m):
    return (n + m - 1) // m * m


def _row_tile(m, pref):
    return pref if m % pref == 0 else m


def _rows_plan(m, pref):
    if m <= RESIDENT_ROWS or m % pref:
        return m, pl.Buffered(1)
    return pref, None


def _rms_scale(x):
    return lax.rsqrt(jnp.mean(x * x, axis=-1, keepdims=True) + RMS_EPS)


def _dot(a, b):
    return jnp.dot(a, b, preferred_element_type=F32)


def _dot_nt(a, b):
    return lax.dot_general(a, b, (((1,), (1,)), ((), ())), preferred_element_type=F32)


def _dot_tn(a, b):
    return lax.dot_general(a, b, (((0,), (0,)), ((), ())), preferred_element_type=F32)


def _split2(x):
    hi = x.astype(BF16)
    lo = (x - hi.astype(F32)).astype(BF16)
    return hi, lo


def _head_sum(x, bd):
    w = bd.shape[0]
    rows = x.shape[0]
    hi, lo = _split2(x)
    starts = range(0, x.shape[1], w)
    stacked = jnp.concatenate([part[:, s:s + w] for s in starts for part in (hi, lo)], axis=0)
    sums = _dot(stacked, bd)
    parts = [sums[2 * i * rows:(2 * i + 1) * rows] + sums[(2 * i + 1) * rows:(2 * i + 2) * rows]
             for i in range(len(starts))]
    return jnp.concatenate(parts, axis=1)


def _ffn_body(x_ref, gpre_ref, wg_ref, wu_ref, wd_ref, *rest, nb_main, has_tail):
    if has_tail:
        wgt_ref, wut_ref, wdt_ref, gpost_ref, o_ref, xn_ref = rest
    else:
        gpost_ref, o_ref, xn_ref = rest
    j = pl.program_id(1)

    @pl.when(j == 0)
    def _init():
        x = x_ref[...]
        xn_ref[...] = (x * _rms_scale(x) * gpre_ref[...]).astype(BF16)
        o_ref[...] = jnp.zeros_like(o_ref)

    def accumulate(wg, wu, wd):
        xn = xn_ref[...]
        gate = _dot(xn, wg[...])
        up = _dot(xn, wu[...])
        hmid = (gate * jax.nn.sigmoid(gate) * up).astype(BF16)
        o_ref[...] += _dot(hmid, wd[...])

    if has_tail:
        pl.when(j < nb_main)(lambda: accumulate(wg_ref, wu_ref, wd_ref))
        pl.when(j == nb_main)(lambda: accumulate(wgt_ref, wut_ref, wdt_ref))
    else:
        accumulate(wg_ref, wu_ref, wd_ref)

    @pl.when(j == pl.num_programs(1) - 1)
    def _fin():
        a = o_ref[...]
        o_ref[...] = x_ref[...] + 0.5 * (a * _rms_scale(a) * gpost_ref[...])


def _ffn_vmem_bytes(tm, d, tf, f, row_buffers):
    rows = tm * d * (4 * row_buffers + 4 * row_buffers + 2)
    weights = 3 * d * tf * 2 * 2 + 3 * d * (f % tf) * 2
    temps = tm * tf * (4 + 4 + 2)
    return rows + weights + temps


def _ffn(x, g_pre, wg, wu, wd, g_post):
    m, d = x.shape
    f = wg.shape[1]
    tm, rows_mode = _rows_plan(m, TILES["ffn_tm"])
    row_buffers = 1 if rows_mode is not None else 2
    tf = next((t for t in TILES["ffn_tf"] if _ffn_vmem_bytes(tm, d, t, f, row_buffers) <= VMEM_BUDGET),
              TILES["ffn_tf"][-1])
    tf = min(tf, f)
    nb_main = f // tf
    f_main = nb_main * tf
    assert (f - f_main) % LANES == 0
    has_tail = f_main < f
    last_main = nb_main - 1

    def whole(arr):
        return pl.BlockSpec(arr.shape, lambda i, j: (0, 0), pipeline_mode=pl.Buffered(1))

    tails = [wg[:, f_main:], wu[:, f_main:], wd[f_main:, :]] if has_tail else []
    return pl.pallas_call(
        functools.partial(_ffn_body, nb_main=nb_main, has_tail=has_tail),
        out_shape=jax.ShapeDtypeStruct((m, d), F32),
        grid=(m // tm, nb_main + has_tail),
        in_specs=[
            pl.BlockSpec((tm, d), lambda i, j: (i, 0), pipeline_mode=rows_mode),
            pl.BlockSpec((1, d), lambda i, j: (0, 0)),
            pl.BlockSpec((d, tf), lambda i, j: (0, jnp.minimum(j, last_main))),
            pl.BlockSpec((d, tf), lambda i, j: (0, jnp.minimum(j, last_main))),
            pl.BlockSpec((tf, d), lambda i, j: (jnp.minimum(j, last_main), 0)),
        ] + [whole(t) for t in tails] + [pl.BlockSpec((1, d), lambda i, j: (0, 0))],
        out_specs=pl.BlockSpec((tm, d), lambda i, j: (i, 0), pipeline_mode=rows_mode),
        scratch_shapes=[pltpu.VMEM((tm, d), BF16)],
        compiler_params=_cparams(("parallel", "arbitrary")),
        name="ffn",
    )(x, g_pre, wg, wu, wd, *tails, g_post)


def _inproj_body(x_ref, g_ref, w_ref, o_ref, xn_ref):
    @pl.when(pl.program_id(1) == 0)
    def _init():
        x = x_ref[...]
        xn_ref[...] = (x * _rms_scale(x) * g_ref[...]).astype(BF16)

    o_ref[...] = _dot(xn_ref[...], w_ref[...])


def _inproj(x, g, w):
    m, d = x.shape
    n = w.shape[1]
    tm, rows_mode = _rows_plan(m, TILES["inproj_tm"])
    tn = _row_tile(n, TILES["inproj_tn"])
    return pl.pallas_call(
        _inproj_body,
        out_shape=jax.ShapeDtypeStruct((m, n), F32),
        grid=(m // tm, n // tn),
        in_specs=[
            pl.BlockSpec((tm, d), lambda i, j: (i, 0), pipeline_mode=rows_mode),
            pl.BlockSpec((1, d), lambda i, j: (0, 0)),
            pl.BlockSpec((d, tn), lambda i, j: (0, j)),
        ],
        out_specs=pl.BlockSpec((tm, tn), lambda i, j: (i, j)),
        scratch_shapes=[pltpu.VMEM((tm, d), BF16)],
        compiler_params=_cparams(("parallel", "arbitrary")),
        name="inproj",
    )(x, g, w)


def _shift_rows(x, carry_rows):
    n = len(carry_rows)
    out = pltpu.roll(x, n, axis=0)
    row = lax.broadcasted_iota(jnp.int32, x.shape, 0)
    for i, c in enumerate(carry_rows):
        out = jnp.where(row == i, c, out)
    return out


def _bd(y, mask):
    reps = mask.shape[0] // y.shape[0]
    return jnp.where(mask, jnp.concatenate([y] * reps, axis=0), jnp.zeros((), y.dtype))


def _mm_heads(x, y, mask, nt=False):
    dot = _dot_nt if nt else _dot
    return dot(x.astype(BF16), _bd(y.astype(BF16), mask))


def _round_robin(tasks):
    while tasks:
        tasks = [t for t in tasks if next(t, tasks) is not tasks]


def _mixer_body(pb_ref, pc_ref, ph_ref, pr_ref, pk_ref, pv_ref, pl_ref, cprev_ref, sprev_ref, s0_ref,
                cw_ref, mu_ref, w0_ref, wdec_ref, a0_ref, wa_ref, wg_ref, kk_ref, ka_ref, rk_ref,
                lnw_ref, lnb_ref, bd_ref,
                mixc_ref, mixr_ref, nconv_ref, nshift_ref, sout_ref,
                cu_scr, cs_scr, s_scr, r_scr, lw_scr, k_scr, v_scr, kn_scr, b_scr, y_scr, bonus_scr, g_scr):
    ti = pl.program_id(1)
    c = WKV_CHUNK
    n = HEAD_DIM
    rd = pr_ref.shape[-1]
    nchunk = pc_ref.shape[1] // c
    wdt = WKV_GROUP * n
    ngroups = rd // wdt

    @pl.when(ti == 0)
    def _init():
        cu_scr[...] = cprev_ref[0]
        cs_scr[...] = sprev_ref[0]
        s_scr[...] = s0_ref[0]

    bd = bd_ref[...]
    row = lax.broadcasted_iota(jnp.int32, (c, c), 0)
    col = lax.broadcasted_iota(jnp.int32, (c, c), 1)
    ltri = jnp.where(row >= col, 1.0, 0.0).astype(BF16)
    prow = lax.broadcasted_iota(jnp.int32, (c, wdt), 0)
    pcol = lax.broadcasted_iota(jnp.int32, (c, wdt), 1) % n
    strict = prow > pcol
    incl = prow >= pcol
    brow = lax.broadcasted_iota(jnp.int32, (wdt, wdt), 0) // n
    bcol = lax.broadcasted_iota(jnp.int32, (wdt, wdt), 1) // n
    same_head = brow == bcol

    def prep_chunk(sl):
        u = pc_ref[0, sl, :] * ph_ref[0, sl, :]
        c0 = cu_scr[0:1, :]
        c1 = cu_scr[1:2, :]
        u2 = _shift_rows(u, [c0, c1])
        u1 = _shift_rows(u, [c1])
        conv = u2 * cw_ref[0:1, :] + u1 * cw_ref[1:2, :] + u * cw_ref[2:3, :]
        mixc_ref[0, sl, :] = (pb_ref[0, sl, :] * conv).astype(mixc_ref.dtype)
        nr = u.shape[0]
        cu_scr[...] = u[nr - 2:nr, :]
        yield

        def shifted(p, lo, hi):
            prev = _shift_rows(p, [cs_scr[0:1, lo:hi]])
            q = p + (prev - p) * mu_ref[0:1, lo:hi]
            cs_scr[0:1, lo:hi] = p[nr - 1:nr, :]
            return q

        ql = shifted(pl_ref[0, sl, :], 3 * rd, 3 * rd + LORA_BLOCK)
        ql_da = ql[:, 0:LANES]
        dec_in = w0_ref[...] + _dot(jnp.tanh(ql_da).astype(BF16), wdec_ref[...])
        a = jax.nn.sigmoid(a0_ref[...] + _dot(ql_da.astype(BF16), wa_ref[...]))
        g_scr[sl, :] = _dot(jax.nn.sigmoid(ql[:, LANES:LANES + wg_ref.shape[0]]).astype(BF16), wg_ref[...])
        yield
        z = -dec_in
        softplus = jnp.maximum(z, 0.0) + jnp.log(1.0 + jnp.exp(-jnp.abs(z)))
        w_log = -softplus - 0.5
        lw_scr[sl, :] = -jnp.exp(w_log)
        k = shifted(pk_ref[0, sl, :], rd, 2 * rd)
        kk = k * kk_ref[...]
        ssq = _head_sum(kk * kk, bd)
        yield
        kk = kk / jnp.maximum(jnp.sqrt(ssq), L2_EPS)
        kn_scr[sl, :] = kk
        b_scr[sl, :] = kk * a
        k = k * (1.0 + (a - 1.0) * ka_ref[...])
        k_scr[sl, :] = k
        r = shifted(pr_ref[0, sl, :], 0, rd)
        r_scr[sl, :] = r
        rk_sum = _head_sum(r * k * rk_ref[...], bd)
        yield
        v = shifted(pv_ref[0, sl, :], 2 * rd, 3 * rd)
        v_scr[sl, :] = v
        bonus_scr[sl, :] = rk_sum * v

    def wkv_pre(sl, ln, out):
        lw = lw_scr[sl, ln]
        l_hi = lw.astype(BF16)
        rem = lw - l_hi.astype(F32)
        l_mid = rem.astype(BF16)
        l_lo = (rem - l_mid.astype(F32)).astype(BF16)
        cum = _dot(ltri, l_hi) + (_dot(ltri, l_mid) + _dot(ltri, l_lo))
        yield
        cum_last = cum[c - 1:c, :]
        g_end = jnp.exp(cum_last - cum)
        g_inv = jnp.exp(-cum)
        kk = k_scr[sl, ln]
        vv = v_scr[sl, ln]
        bb = b_scr[sl, ln]
        ap = jnp.concatenate([kn_scr[sl, ln] * jnp.exp(cum - lw), r_scr[sl, ln] * jnp.exp(cum)], axis=0)
        g_b = _mm_heads(ap, bb * g_inv, same_head, nt=True)
        g_k = _mm_heads(ap, kk * g_inv, same_head, nt=True)
        yield
        q = jnp.where(strict, -g_b[:c], 0.0)
        lk = jnp.concatenate([jnp.where(strict, g_k[:c], 0.0), jnp.where(incl, g_k[c:], 0.0)], axis=0)
        t1v = _mm_heads(lk, vv, same_head)
        z = q
        q = _mm_heads(q, q, same_head)
        yield
        for level in range(5):
            if level < 4:
                zq_qq = _mm_heads(jnp.concatenate([z, q], axis=0), q, same_head)
                z, q = z + q + zq_qq[:c], zq_qq[c:]
            else:
                z = z + q + _mm_heads(z, q, same_head)
            yield
        out.update(ap=ap.astype(BF16), t1v=t1v, z=z.astype(BF16), irb=jnp.where(incl, g_b[c:], 0.0).astype(BF16),
                   vv=vv, kb=jnp.concatenate([kk * g_end, bb * g_end], axis=0).astype(BF16),
                   g_last=jnp.exp(cum_last))

    def wkv_post(sl, ln, pre):
        s0 = s_scr[:, ln]
        a_s = _dot_nt(pre["ap"], _bd(s0.astype(BF16), same_head))
        yield
        t1 = a_s + pre["t1v"]
        u = t1[:c] + _dot(pre["z"], _bd(t1[:c].astype(BF16), same_head))
        yield
        ub = u.astype(BF16)
        y_scr[sl, ln] = t1[c:] - _dot(pre["irb"], _bd(ub, same_head))
        vu = jnp.concatenate([pre["vv"].astype(BF16), -ub], axis=0)
        gram = jnp.where(same_head, _dot_tn(vu, pre["kb"]), 0.0)
        upd = gram[0:n]
        for h in range(1, WKV_GROUP):
            upd = upd + gram[h * n:(h + 1) * n]
        s_scr[:, ln] = s0 * pre["g_last"] + upd

    def out_chunk(sl):
        y = y_scr[sl, :]
        inv_n = 1.0 / HEAD_DIM
        mean = _head_sum(y, bd) * inv_n
        yield
        yc = y - mean
        var = _head_sum(yc * yc, bd) * inv_n
        yield
        yn = yc * lax.rsqrt(var + GN_EPS) * lnw_ref[...] + lnb_ref[...]
        mixr_ref[0, sl, :] = ((yn + bonus_scr[sl, :]) * g_scr[sl, :]).astype(mixr_ref.dtype)

    def rows(i):
        return slice(i * c, (i + 1) * c)

    def in_turn(gens):
        for gen in gens:
            yield from gen

    lanes = [slice(g * wdt, (g + 1) * wdt) for g in range(ngroups)]
    pre = [[{} for _ in lanes] for _ in range(nchunk)]
    span = PIPE_CHUNKS if nchunk % PIPE_CHUNKS == 0 else 1
    nstage = nchunk // span

    def chunks_of(stage):
        return range(stage * span, (stage + 1) * span) if 0 <= stage < nstage else ()

    for step in range(nstage + 3):
        tasks = [in_turn([wkv_post(rows(i), ln, pre[i][g]) for i in chunks_of(step - 2)])
                 for g, ln in enumerate(lanes)]
        tasks += [wkv_pre(rows(i), ln, pre[i][g]) for i in chunks_of(step - 1) for g, ln in enumerate(lanes)]
        tasks.append(in_turn([prep_chunk(rows(i)) for i in chunks_of(step)]))
        tasks.append(in_turn([out_chunk(rows(i)) for i in chunks_of(step - 3)]))
        _round_robin(tasks)

    @pl.when(ti == pl.num_programs(1) - 1)
    def _fin():
        nconv_ref[0] = cu_scr[...]
        nshift_ref[0] = cs_scr[...]
        sout_ref[0] = s_scr[...]


def _mixer(p3, conv_prev, shift_prev, s0, wts):
    bsz, t, _ = p3.shape
    cc = conv_prev.shape[-1]
    rd = wts["w0"].shape[-1]
    sp = shift_prev.shape[-1]
    tt = _row_tile(t, TILES["mixer_tt"])
    assert tt % WKV_CHUNK == 0 and rd % (WKV_GROUP * HEAD_DIM) == 0
    lora_blk = (3 * cc + 3 * rd) // LORA_BLOCK

    def col(width, idx):
        return pl.BlockSpec((1, tt, width), lambda b, i: (b, i, idx))

    def const(arr):
        return pl.BlockSpec(arr.shape, lambda b, i: (0,) * arr.ndim)

    def per_stream(rows, width):
        return pl.BlockSpec((1, rows, width), lambda b, i: (b, 0, 0))

    names = ("conv_w", "mu", "w0", "wdec", "a0", "wa", "wg", "k_k", "k_a", "r_k", "ln_w", "ln_b", "bd")
    consts = [wts[n] for n in names]
    tok = pltpu.VMEM((tt, rd), F32)
    return pl.pallas_call(
        _mixer_body,
        out_shape=[jax.ShapeDtypeStruct((bsz, t, cc), BF16), jax.ShapeDtypeStruct((bsz, t, rd), BF16),
                   jax.ShapeDtypeStruct((bsz, 2, cc), F32), jax.ShapeDtypeStruct((bsz, 1, sp), F32),
                   jax.ShapeDtypeStruct((bsz, HEAD_DIM, rd), F32)],
        grid=(bsz, t // tt),
        in_specs=[col(cc, 0), col(cc, 1), col(cc, 2), col(rd, 3), col(rd, 4), col(rd, 5),
                  col(LORA_BLOCK, lora_blk),
                  per_stream(2, cc), per_stream(1, sp), per_stream(HEAD_DIM, rd)]
        + [const(c) for c in consts],
        out_specs=[pl.BlockSpec((1, tt, cc), lambda b, i: (b, i, 0)), pl.BlockSpec((1, tt, rd), lambda b, i: (b, i, 0)),
                   per_stream(2, cc), per_stream(1, sp), per_stream(HEAD_DIM, rd)],
        scratch_shapes=[pltpu.VMEM((2, cc), F32), pltpu.VMEM((1, sp), F32), pltpu.VMEM((HEAD_DIM, rd), F32)]
        + [tok] * 9,
        compiler_params=_cparams(("parallel", "arbitrary")),
        name="mixer",
    )(p3, p3, p3, p3, p3, p3, p3, conv_prev, shift_prev, s0, *consts)


def _outproj_body(h_ref, mc_ref, mr_ref, wo_ref, gpost_ref, o_ref):
    cc = mc_ref.shape[-1]
    m = _dot(mc_ref[...], wo_ref[0:cc, :]) + _dot(mr_ref[...], wo_ref[cc:, :])
    o_ref[...] = h_ref[...] + m * _rms_scale(m) * gpost_ref[...]


def _outproj(h, mix_c, mix_r, wo, g_post):
    m, d = h.shape
    tm = _row_tile(m, TILES["outproj_tm"])

    def rows(width):
        return pl.BlockSpec((tm, width), lambda i: (i, 0))

    return pl.pallas_call(
        _outproj_body,
        out_shape=jax.ShapeDtypeStruct((m, d), F32),
        grid=(m // tm,),
        in_specs=[rows(d), rows(mix_c.shape[1]), rows(mix_r.shape[1]),
                  pl.BlockSpec(wo.shape, lambda i: (0, 0), pipeline_mode=pl.Buffered(1)),
                  pl.BlockSpec(g_post.shape, lambda i: (0, 0))],
        out_specs=rows(d),
        compiler_params=_cparams(("parallel",)),
        name="outproj",
    )(h, mix_c, mix_r, wo, g_post)


def _cast_pad(w, axis, extra):
    shape = list(w.shape)
    shape[axis] = extra
    return jnp.concatenate([w.astype(BF16), jnp.zeros(shape, BF16)], axis=axis)


def _prepare_weights(g_ffn1_pre, w_ffn1_gate, w_ffn1_up, w_ffn1_down, g_ffn1_post, g_mix_pre, w_in, conv_w,
                     tshift_mu, w0, w_decay_up, a0, w_a_up, w_g_up, k_k, k_a, r_k, ln_x_w, ln_x_b, w_out,
                     g_mix_post, g_ffn2_pre, w_ffn2_gate, w_ffn2_up, w_ffn2_down, g_ffn2_post):
    rd = w0.shape[-1]
    dl, al, gl = w_decay_up.shape[0], w_a_up.shape[0], w_g_up.shape[0]
    assert dl + al == LANES and gl <= LORA_BLOCK - LANES and rd % HEAD_DIM == 0
    lora = dl + al + gl
    gl_pad = _round_up(gl, LANES)

    def row(vec):
        return vec.reshape(1, -1).astype(F32)

    def ffn_w(wg, wu, wd):
        return wg.astype(BF16), wu.astype(BF16), wd.astype(BF16)

    head = jnp.arange(MXU_DIM, dtype=jnp.int32) // HEAD_DIM
    return dict(
        ffn1=(row(g_ffn1_pre),) + ffn_w(w_ffn1_gate, w_ffn1_up, w_ffn1_down) + (row(g_ffn1_post),),
        ffn2=(row(g_ffn2_pre),) + ffn_w(w_ffn2_gate, w_ffn2_up, w_ffn2_down) + (row(g_ffn2_post),),
        g_mix_pre=row(g_mix_pre),
        w_in=_cast_pad(w_in, 1, LORA_BLOCK - lora),
        lora=lora,
        conv_w=conv_w.astype(F32),
        mu=jnp.pad(row(tshift_mu), ((0, 0), (0, LORA_BLOCK - lora))),
        w0=row(w0),
        wdec=jnp.pad(w_decay_up.astype(BF16), ((0, al), (0, 0))),
        a0=row(a0),
        wa=jnp.pad(w_a_up.astype(BF16), ((dl, 0), (0, 0))),
        wg=jnp.pad(w_g_up.astype(BF16), ((0, gl_pad - gl), (0, 0))),
        k_k=row(k_k), k_a=row(k_a), r_k=row(r_k),
        bd=(head[:, None] == head[None, :]).astype(BF16),
        ln_w=row(ln_x_w), ln_b=row(ln_x_b),
        w_out=w_out.astype(BF16),
        g_mix_post=row(g_mix_post),
    )


def _layer(x3, conv_prev, shift_prev, wkv_prev, wts):
    bsz, t, d = x3.shape
    m = bsz * t
    lora = wts["lora"]
    x = x3.reshape(m, d)
    h = _ffn(x, *wts["ffn1"])
    p = _inproj(h, wts["g_mix_pre"], wts["w_in"])
    shift_pad = jnp.pad(shift_prev.astype(F32), ((0, 0), (0, 0), (0, LORA_BLOCK - lora)))
    nh, hd = wkv_prev.shape[1], wkv_prev.shape[2]
    s0 = wkv_prev.astype(F32).transpose(0, 2, 1, 3).reshape(bsz, hd, nh * hd)
    mix_c, mix_r, new_conv, new_shift, s_new = _mixer(
        p.reshape(bsz, t, -1), conv_prev.astype(F32), shift_pad, s0, wts)
    new_wkv = s_new.reshape(bsz, hd, nh, hd).transpose(0, 2, 1, 3)
    h = _outproj(h, mix_c.reshape(m, -1), mix_r.reshape(m, -1), wts["w_out"], wts["g_mix_post"])
    out = _ffn(h, *wts["ffn2"])
    shift_dim = shift_prev.shape[-1]
    return out.reshape(bsz, t, d), new_conv, new_shift[:, :, :shift_dim], new_wkv


def kernel(x_prompt, x_sample, cache_conv, cache_shift, state_wkv, g_ffn1_pre, w_ffn1_gate, w_ffn1_up, w_ffn1_down, g_ffn1_post, g_mix_pre, w_in, conv_w, tshift_mu, w0, w_decay_up, a0, w_a_up, w_g_up, k_k, k_a, r_k, ln_x_w, ln_x_b, w_out, g_mix_post, g_ffn2_pre, w_ffn2_gate, w_ffn2_up, w_ffn2_down, g_ffn2_post):
    depth = w_in.shape[0]
    bp = x_prompt.shape[0]
    conv_rows, conv_ch = cache_conv.shape[2], cache_conv.shape[3]
    shift_dim = cache_shift.shape[-1]
    nh, hd = state_wkv.shape[2], state_wkv.shape[3]
    per_layer = (g_ffn1_pre, w_ffn1_gate, w_ffn1_up, w_ffn1_down, g_ffn1_post, g_mix_pre, w_in, conv_w,
                 tshift_mu, w0, w_decay_up, a0, w_a_up, w_g_up, k_k, k_a, r_k, ln_x_w, ln_x_b, w_out,
                 g_mix_post, g_ffn2_pre, w_ffn2_gate, w_ffn2_up, w_ffn2_down, g_ffn2_post)
    yp, ys = x_prompt, x_sample
    outs = [[] for _ in range(6)]
    for l in range(depth):
        wts = _prepare_weights(*(w[l] for w in per_layer))
        zc = jnp.zeros((bp, conv_rows, conv_ch), F32)
        zs = jnp.zeros((bp, 1, shift_dim), F32)
        zw = jnp.zeros((bp, nh, hd, hd), F32)
        yp, c1, s1, w1 = _layer(yp, zc, zs, zw, wts)
        ys, c2, s2, w2 = _layer(ys, cache_conv[l], cache_shift[l], state_wkv[l], wts)
        for lst, val, ref in zip(outs, (c1, s1, w1, c2, s2, w2),
                                 (cache_conv, cache_shift, state_wkv, cache_conv, cache_shift, state_wkv)):
            lst.append(val.astype(ref.dtype))
    return (yp, ys) + tuple(jnp.stack(o, 0) for o in outs)
```

```python
import functools

import jax
import jax.numpy as jnp
from jax import lax
from jax.experimental import pallas as pl
from jax.experimental.pallas import tpu as pltpu

F32 = jnp.float32
BF16 = jnp.bfloat16

RMS_EPS = 1e-6
GN_EPS = 64e-5
L2_EPS = 1e-12
HEAD_DIM = 64
WKV_CHUNK = 64
MXU_DIM = 256
WKV_GROUP = MXU_DIM // HEAD_DIM
PIPE_CHUNKS = 2
LANES = 128
LORA_BLOCK = 512
VMEM_LIMIT = 56 * 1024 * 1024
VMEM_BUDGET = VMEM_LIMIT - 4 * 1024 * 1024
TILES = dict(ffn_tm=1024, ffn_tf=(256,), inproj_tm=1024, inproj_tn=1664, mixer_tt=256, outproj_tm=512)
RESIDENT_ROWS = 1024


def _cparams(sem):
    return pltpu.CompilerParams(dimension_semantics=sem, vmem_limit_bytes=VMEM_LIMIT)


def _round_up(n, m):
    return (n + m - 1) // m * m


def _row_tile(m, pref):
    return pref if m % pref == 0 else m


def _rows_plan(m, pref):
    if m <= RESIDENT_ROWS or m % pref:
        return m, pl.Buffered(1)
    return pref, None


def _rms_scale(x):
    return lax.rsqrt(jnp.mean(x * x, axis=-1, keepdims=True) + RMS_EPS)


def _dot(a, b):
    return jnp.dot(a, b, preferred_element_type=F32)


def _dot_nt(a, b):
    return lax.dot_general(a, b, (((1,), (1,)), ((), ())), preferred_element_type=F32)


def _dot_tn(a, b):
    return lax.dot_general(a, b, (((0,), (0,)), ((), ())), preferred_element_type=F32)


def _split2(x):
    hi = x.astype(BF16)
    lo = (x - hi.astype(F32)).astype(BF16)
    return hi, lo


def _head_sum(x, bd):
    w = bd.shape[0]
    rows = x.shape[0]
    hi, lo = _split2(x)
    starts = range(0, x.shape[1], w)
    stacked = jnp.concatenate([part[:, s:s + w] for s in starts for part in (hi, lo)], axis=0)
    sums = _dot(stacked, bd)
    parts = [sums[2 * i * rows:(2 * i + 1) * rows] + sums[(2 * i + 1) * rows:(2 * i + 2) * rows]
             for i in range(len(starts))]
    return jnp.concatenate(parts, axis=1)


def _ffn_body(x_ref, gpre_ref, wg_ref, wu_ref, wd_ref, *rest, nb_main, has_tail):
    if has_tail:
        wgt_ref, wut_ref, wdt_ref, gpost_ref, o_ref, xn_ref = rest
    else:
        gpost_ref, o_ref, xn_ref = rest
    j = pl.program_id(1)

    @pl.when(j == 0)
    def _init():
        x = x_ref[...]
        xn_ref[...] = (x * _rms_scale(x) * gpre_ref[...]).astype(BF16)
        o_ref[...] = jnp.zeros_like(o_ref)

    def accumulate(wg, wu, wd):
        xn = xn_ref[...]
        gate = _dot(xn, wg[...])
        up = _dot(xn, wu[...])
        hmid = (gate * jax.nn.sigmoid(gate) * up).astype(BF16)
        o_ref[...] += _dot(hmid, wd[...])

    if has_tail:
        pl.when(j < nb_main)(lambda: accumulate(wg_ref, wu_ref, wd_ref))
        pl.when(j == nb_main)(lambda: accumulate(wgt_ref, wut_ref, wdt_ref))
    else:
        accumulate(wg_ref, wu_ref, wd_ref)

    @pl.when(j == pl.num_programs(1) - 1)
    def _fin():
        a = o_ref[...]
        o_ref[...] = x_ref[...] + 0.5 * (a * _rms_scale(a) * gpost_ref[...])


def _ffn_vmem_bytes(tm, d, tf, f, row_buffers):
    rows = tm * d * (4 * row_buffers + 4 * row_buffers + 2)
    weights = 3 * d * tf * 2 * 2 + 3 * d * (f % tf) * 2
    temps = tm * tf * (4 + 4 + 2)
    return rows + weights + temps


def _ffn(x, g_pre, wg, wu, wd, g_post):
    m, d = x.shape
    f = wg.shape[1]
    tm, rows_mode = _rows_plan(m, TILES["ffn_tm"])
    row_buffers = 1 if rows_mode is not None else 2
    tf = next((t for t in TILES["ffn_tf"] if _ffn_vmem_bytes(tm, d, t, f, row_buffers) <= VMEM_BUDGET),
              TILES["ffn_tf"][-1])
    tf = min(tf, f)
    nb_main = f // tf
    f_main = nb_main * tf
    assert (f - f_main) % LANES == 0
    has_tail = f_main < f
    last_main = nb_main - 1

    def whole(arr):
        return pl.BlockSpec(arr.shape, lambda i, j: (0, 0), pipeline_mode=pl.Buffered(1))

    tails = [wg[:, f_main:], wu[:, f_main:], wd[f_main:, :]] if has_tail else []
    return pl.pallas_call(
        functools.partial(_ffn_body, nb_main=nb_main, has_tail=has_tail),
        out_shape=jax.ShapeDtypeStruct((m, d), F32),
        grid=(m // tm, nb_main + has_tail),
        in_specs=[
            pl.BlockSpec((tm, d), lambda i, j: (i, 0), pipeline_mode=rows_mode),
            pl.BlockSpec((1, d), lambda i, j: (0, 0)),
            pl.BlockSpec((d, tf), lambda i, j: (0, jnp.minimum(j, last_main))),
            pl.BlockSpec((d, tf), lambda i, j: (0, jnp.minimum(j, last_main))),
            pl.BlockSpec((tf, d), lambda i, j: (jnp.minimum(j, last_main), 0)),
        ] + [whole(t) for t in tails] + [pl.BlockSpec((1, d), lambda i, j: (0, 0))],
        out_specs=pl.BlockSpec((tm, d), lambda i, j: (i, 0), pipeline_mode=rows_mode),
        scratch_shapes=[pltpu.VMEM((tm, d), BF16)],
        compiler_params=_cparams(("parallel", "arbitrary")),
        name="ffn",
    )(x, g_pre, wg, wu, wd, *tails, g_post)


def _inproj_body(x_ref, g_ref, w_ref, o_ref, xn_ref):
    @pl.when(pl.program_id(1) == 0)
    def _init():
        x = x_ref[...]
        xn_ref[...] = (x * _rms_scale(x) * g_ref[...]).astype(BF16)

    o_ref[...] = _dot(xn_ref[...], w_ref[...])


def _inproj(x, g, w):
    m, d = x.shape
    n = w.shape[1]
    tm, rows_mode = _rows_plan(m, TILES["inproj_tm"])
    tn = _row_tile(n, TILES["inproj_tn"])
    return pl.pallas_call(
        _inproj_body,
        out_shape=jax.ShapeDtypeStruct((m, n), F32),
        grid=(m // tm, n // tn),
        in_specs=[
            pl.BlockSpec((tm, d), lambda i, j: (i, 0), pipeline_mode=rows_mode),
            pl.BlockSpec((1, d), lambda i, j: (0, 0)),
            pl.BlockSpec((d, tn), lambda i, j: (0, j)),
        ],
        out_specs=pl.BlockSpec((tm, tn), lambda i, j: (i, j)),
        scratch_shapes=[pltpu.VMEM((tm, d), BF16)],
        compiler_params=_cparams(("parallel", "arbitrary")),
        name="inproj",
    )(x, g, w)


def _shift_rows(x, carry_rows):
    n = len(carry_rows)
    out = pltpu.roll(x, n, axis=0)
    row = lax.broadcasted_iota(jnp.int32, x.shape, 0)
    for i, c in enumerate(carry_rows):
        out = jnp.where(row == i, c, out)
    return out


def _bd(y, mask):
    reps = mask.shape[0] // y.shape[0]
    return jnp.where(mask, jnp.concatenate([y] * reps, axis=0), jnp.zeros((), y.dtype))


def _mm_heads(x, y, mask, nt=False):
    dot = _dot_nt if nt else _dot
    return dot(x.astype(BF16), _bd(y.astype(BF16), mask))


def _round_robin(tasks):
    while tasks:
        tasks = [t for t in tasks if next(t, tasks) is not tasks]


def _mixer_body(pb_ref, pc_ref, ph_ref, pr_ref, pk_ref, pv_ref, pl_ref, cprev_ref, sprev_ref, s0_ref,
                cw_ref, mu_ref, w0_ref, wdec_ref, a0_ref, wa_ref, wg_ref, kk_ref, ka_ref, rk_ref,
                lnw_ref, lnb_ref, bd_ref,
                mixc_ref, mixr_ref, nconv_ref, nshift_ref, sout_ref,
                cu_scr, cs_scr, s_scr, r_scr, lw_scr, k_scr, v_scr, kn_scr, b_scr, y_scr, bonus_scr, g_scr):
    ti = pl.program_id(1)
    c = WKV_CHUNK
    n = HEAD_DIM
    rd = pr_ref.shape[-1]
    nchunk = pc_ref.shape[1] // c
    wdt = WKV_GROUP * n
    ngroups = rd // wdt

    @pl.when(ti == 0)
    def _init():
        cu_scr[...] = cprev_ref[0]
        cs_scr[...] = sprev_ref[0]
        s_scr[...] = s0_ref[0]

    bd = bd_ref[...]
    row = lax.broadcasted_iota(jnp.int32, (c, c), 0)
    col = lax.broadcasted_iota(jnp.int32, (c, c), 1)
    ltri = jnp.where(row >= col, 1.0, 0.0).astype(BF16)
    prow = lax.broadcasted_iota(jnp.int32, (c, wdt), 0)
    pcol = lax.broadcasted_iota(jnp.int32, (c, wdt), 1) % n
    strict = prow > pcol
    incl = prow >= pcol
    brow = lax.broadcasted_iota(jnp.int32, (wdt, wdt), 0) // n
    bcol = lax.broadcasted_iota(jnp.int32, (wdt, wdt), 1) // n
    same_head = brow == bcol

    def prep_chunk(sl):
        u = pc_ref[0, sl, :] * ph_ref[0, sl, :]
        c0 = cu_scr[0:1, :]
        c1 = cu_scr[1:2, :]
        u2 = _shift_rows(u, [c0, c1])
        u1 = _shift_rows(u, [c1])
        conv = u2 * cw_ref[0:1, :] + u1 * cw_ref[1:2, :] + u * cw_ref[2:3, :]
        mixc_ref[0, sl, :] = (pb_ref[0, sl, :] * conv).astype(mixc_ref.dtype)
        nr = u.shape[0]
        cu_scr[...] = u[nr - 2:nr, :]
        yield

        def shifted(p, lo, hi):
            prev = _shift_rows(p, [cs_scr[0:1, lo:hi]])
            q = p + (prev - p) * mu_ref[0:1, lo:hi]
            cs_scr[0:1, lo:hi] = p[nr - 1:nr, :]
            return q

        ql = shifted(pl_ref[0, sl, :], 3 * rd, 3 * rd + LORA_BLOCK)
        ql_da = ql[:, 0:LANES]
        dec_in = w0_ref[...] + _dot(jnp.tanh(ql_da).astype(BF16), wdec_ref[...])
        a = jax.nn.sigmoid(a0_ref[...] + _dot(ql_da.astype(BF16), wa_ref[...]))
        g_scr[sl, :] = _dot(jax.nn.sigmoid(ql[:, LANES:LANES + wg_ref.shape[0]]).astype(BF16), wg_ref[...])
        yield
        z = -dec_in
        softplus = jnp.maximum(z, 0.0) + jnp.log(1.0 + jnp.exp(-jnp.abs(z)))
        w_log = -softplus - 0.5
        lw_scr[sl, :] = -jnp.exp(w_log)
        k = shifted(pk_ref[0, sl, :], rd, 2 * rd)
        kk = k * kk_ref[...]
        ssq = _head_sum(kk * kk, bd)
        yield
        kk = kk / jnp.maximum(jnp.sqrt(ssq), L2_EPS)
        kn_scr[sl, :] = kk
        b_scr[sl, :] = kk * a
        k = k * (1.0 + (a - 1.0) * ka_ref[...])
        k_scr[sl, :] = k
        r = shifted(pr_ref[0, sl, :], 0, rd)
        r_scr[sl, :] = r
        rk_sum = _head_sum(r * k * rk_ref[...], bd)
        yield
        v = shifted(pv_ref[0, sl, :], 2 * rd, 3 * rd)
        v_scr[sl, :] = v
        bonus_scr[sl, :] = rk_sum * v

    def wkv_pre(sl, ln, out):
        lw = lw_scr[sl, ln]
        l_hi = lw.astype(BF16)
        rem = lw - l_hi.astype(F32)
        l_mid = rem.astype(BF16)
        l_lo = (rem - l_mid.astype(F32)).astype(BF16)
        cum = _dot(ltri, l_hi) + (_dot(ltri, l_mid) + _dot(ltri, l_lo))
        yield
        cum_last = cum[c - 1:c, :]
        g_end = jnp.exp(cum_last - cum)
        g_inv = jnp.exp(-cum)
        kk = k_scr[sl, ln]
        vv = v_scr[sl, ln]
        bb = b_scr[sl, ln]
        ap = jnp.concatenate([kn_scr[sl, ln] * jnp.exp(cum - lw), r_scr[sl, ln] * jnp.exp(cum)], axis=0)
        g_b = _mm_heads(ap, bb * g_inv, same_head, nt=True)
        g_k = _mm_heads(ap, kk * g_inv, same_head, nt=True)
        yield
        q = jnp.where(strict, -g_b[:c], 0.0)
        lk = jnp.concatenate([jnp.where(strict, g_k[:c], 0.0), jnp.where(incl, g_k[c:], 0.0)], axis=0)
        t1v = _mm_heads(lk, vv, same_head)
        z = q
        q = _mm_heads(q, q, same_head)
        yield
        for level in range(5):
            if level < 4:
                zq_qq = _mm_heads(jnp.concatenate([z, q], axis=0), q, same_head)
                z, q = z + q + zq_qq[:c], zq_qq[c:]
            else:
                z = z + q + _mm_heads(z, q, same_head)
            yield
        out.update(ap=ap.astype(BF16), t1v=t1v, z=z.astype(BF16), irb=jnp.where(incl, g_b[c:], 0.0).astype(BF16),
                   vv=vv, kb=jnp.concatenate([kk * g_end, bb * g_end], axis=0).astype(BF16),
                   g_last=jnp.exp(cum_last))

    def wkv_post(sl, ln, pre):
        s0 = s_scr[:, ln]
        a_s = _dot_nt(pre["ap"], _bd(s0.astype(BF16), same_head))
        yield
        t1 = a_s + pre["t1v"]
        u = t1[:c] + _dot(pre["z"], _bd(t1[:c].astype(BF16), same_head))
        yield
        ub = u.astype(BF16)
        y_scr[sl, ln] = t1[c:] - _dot(pre["irb"], _bd(ub, same_head))
        vu = jnp.concatenate([pre["vv"].astype(BF16), -ub], axis=0)
        gram = jnp.where(same_head, _dot_tn(vu, pre["kb"]), 0.0)
        upd = gram[0:n]
        for h in range(1, WKV_GROUP):
            upd = upd + gram[h * n:(h + 1) * n]
        s_scr[:, ln] = s0 * pre["g_last"] + upd

    def out_chunk(sl):
        y = y_scr[sl, :]
        inv_n = 1.0 / HEAD_DIM
        mean = _head_sum(y, bd) * inv_n
        yield
        yc = y - mean
        var = _head_sum(yc * yc, bd) * inv_n
        yield
        yn = yc * lax.rsqrt(var + GN_EPS) * lnw_ref[...] + lnb_ref[...]
        mixr_ref[0, sl, :] = ((yn + bonus_scr[sl, :]) * g_scr[sl, :]).astype(mixr_ref.dtype)

    def rows(i):
        return slice(i * c, (i + 1) * c)

    def in_turn(gens):
        for gen in gens:
            yield from gen

    lanes = [slice(g * wdt, (g + 1) * wdt) for g in range(ngroups)]
    pre = [[{} for _ in lanes] for _ in range(nchunk)]
    span = PIPE_CHUNKS if nchunk % PIPE_CHUNKS == 0 else 1
    nstage = nchunk // span

    def chunks_of(stage):
        return range(stage * span, (stage + 1) * span) if 0 <= stage < nstage else ()

    for step in range(nstage + 3):
        tasks = [in_turn([wkv_post(rows(i), ln, pre[i][g]) for i in chunks_of(step - 2)])
                 for g, ln in enumerate(lanes)]
        tasks += [wkv_pre(rows(i), ln, pre[i][g]) for i in chunks_of(step - 1) for g, ln in enumerate(lanes)]
        tasks.append(in_turn([prep_chunk(rows(i)) for i in chunks_of(step)]))
        tasks.append(in_turn([out_chunk(rows(i)) for i in chunks_of(step - 3)]))
        _round_robin(tasks)

    @pl.when(ti == pl.num_programs(1) - 1)
    def _fin():
        nconv_ref[0] = cu_scr[...]
        nshift_ref[0] = cs_scr[...]
        sout_ref[0] = s_scr[...]


def _mixer(p3, conv_prev, shift_prev, s0, wts):
    bsz, t, _ = p3.shape
    cc = conv_prev.shape[-1]
    rd = wts["w0"].shape[-1]
    sp = shift_prev.shape[-1]
    tt = _row_tile(t, TILES["mixer_tt"])
    assert tt % WKV_CHUNK == 0 and rd % (WKV_GROUP * HEAD_DIM) == 0
    lora_blk = (3 * cc + 3 * rd) // LORA_BLOCK

    def col(width, idx):
        return pl.BlockSpec((1, tt, width), lambda b, i: (b, i, idx))

    def const(arr):
        return pl.BlockSpec(arr.shape, lambda b, i: (0,) * arr.ndim)

    def per_stream(rows, width):
        return pl.BlockSpec((1, rows, width), lambda b, i: (b, 0, 0))

    names = ("conv_w", "mu", "w0", "wdec", "a0", "wa", "wg", "k_k", "k_a", "r_k", "ln_w", "ln_b", "bd")
    consts = [wts[n] for n in names]
    tok = pltpu.VMEM((tt, rd), F32)
    return pl.pallas_call(
        _mixer_body,
        out_shape=[jax.ShapeDtypeStruct((bsz, t, cc), BF16), jax.ShapeDtypeStruct((bsz, t, rd), BF16),
                   jax.ShapeDtypeStruct((bsz, 2, cc), F32), jax.ShapeDtypeStruct((bsz, 1, sp), F32),
                   jax.ShapeDtypeStruct((bsz, HEAD_DIM, rd), F32)],
        grid=(bsz, t // tt),
        in_specs=[col(cc, 0), col(cc, 1), col(cc, 2), col(rd, 3), col(rd, 4), col(rd, 5),
                  col(LORA_BLOCK, lora_blk),
                  per_stream(2, cc), per_stream(1, sp), per_stream(HEAD_DIM, rd)]
        + [const(c) for c in consts],
        out_specs=[pl.BlockSpec((1, tt, cc), lambda b, i: (b, i, 0)), pl.BlockSpec((1, tt, rd), lambda b, i: (b, i, 0)),
                   per_stream(2, cc), per_stream(1, sp), per_stream(HEAD_DIM, rd)],
        scratch_shapes=[pltpu.VMEM((2, cc), F32), pltpu.VMEM((1, sp), F32), pltpu.VMEM((HEAD_DIM, rd), F32)]
        + [tok] * 9,
        compiler_params=_cparams(("parallel", "arbitrary")),
        name="mixer",
    )(p3, p3, p3, p3, p3, p3, p3, conv_prev, shift_prev, s0, *consts)


def _outproj_body(h_ref, mc_ref, mr_ref, wo_ref, gpost_ref, o_ref):
    cc = mc_ref.shape[-1]
    m = _dot(mc_ref[...], wo_ref[0:cc, :]) + _dot(mr_ref[...], wo_ref[cc:, :])
    o_ref[...] = h_ref[...] + m * _rms_scale(m) * gpost_ref[...]


def _outproj(h, mix_c, mix_r, wo, g_post):
    m, d = h.shape
    tm = _row_tile(m, TILES["outproj_tm"])

    def rows(width):
        return pl.BlockSpec((tm, width), lambda i: (i, 0))

    return pl.pallas_call(
        _outproj_body,
        out_shape=jax.ShapeDtypeStruct((m, d), F32),
        grid=(m // tm,),
        in_specs=[rows(d), rows(mix_c.shape[1]), rows(mix_r.shape[1]),
                  pl.BlockSpec(wo.shape, lambda i: (0, 0), pipeline_mode=pl.Buffered(1)),
                  pl.BlockSpec(g_post.shape, lambda i: (0, 0))],
        out_specs=rows(d),
        compiler_params=_cparams(("parallel",)),
        name="outproj",
    )(h, mix_c, mix_r, wo, g_post)


def _cast_pad(w, axis, extra):
    shape = list(w.shape)
    shape[axis] = extra
    return jnp.concatenate([w.astype(BF16), jnp.zeros(shape, BF16)], axis=axis)


def _prepare_weights(g_ffn1_pre, w_ffn1_gate, w_ffn1_up, w_ffn1_down, g_ffn1_post, g_mix_pre, w_in, conv_w,
                     tshift_mu, w0, w_decay_up, a0, w_a_up, w_g_up, k_k, k_a, r_k, ln_x_w, ln_x_b, w_out,
                     g_mix_post, g_ffn2_pre, w_ffn2_gate, w_ffn2_up, w_ffn2_down, g_ffn2_post):
    rd = w0.shape[-1]
    dl, al, gl = w_decay_up.shape[0], w_a_up.shape[0], w_g_up.shape[0]
    assert dl + al == LANES and gl <= LORA_BLOCK - LANES and rd % HEAD_DIM == 0
    lora = dl + al + gl
    gl_pad = _round_up(gl, LANES)

    def row(vec):
        return vec.reshape(1, -1).astype(F32)

    def ffn_w(wg, wu, wd):
        return wg.astype(BF16), wu.astype(BF16), wd.astype(BF16)

    head = jnp.arange(MXU_DIM, dtype=jnp.int32) // HEAD_DIM
    return dict(
        ffn1=(row(g_ffn1_pre),) + ffn_w(w_ffn1_gate, w_ffn1_up, w_ffn1_down) + (row(g_ffn1_post),),
        ffn2=(row(g_ffn2_pre),) + ffn_w(w_ffn2_gate, w_ffn2_up, w_ffn2_down) + (row(g_ffn2_post),),
        g_mix_pre=row(g_mix_pre),
        w_in=_cast_pad(w_in, 1, LORA_BLOCK - lora),
        lora=lora,
        conv_w=conv_w.astype(F32),
        mu=jnp.pad(row(tshift_mu), ((0, 0), (0, LORA_BLOCK - lora))),
        w0=row(w0),
        wdec=jnp.pad(w_decay_up.astype(BF16), ((0, al), (0, 0))),
        a0=row(a0),
        wa=jnp.pad(w_a_up.astype(BF16), ((dl, 0), (0, 0))),
        wg=jnp.pad(w_g_up.astype(BF16), ((0, gl_pad - gl), (0, 0))),
        k_k=row(k_k), k_a=row(k_a), r_k=row(r_k),
        bd=(head[:, None] == head[None, :]).astype(BF16),
        ln_w=row(ln_x_w), ln_b=row(ln_x_b),
        w_out=w_out.astype(BF16),
        g_mix_post=row(g_mix_post),
    )


def _layer(x3, conv_prev, shift_prev, wkv_prev, wts):
    bsz, t, d = x3.shape
    m = bsz * t
    lora = wts["lora"]
    x = x3.reshape(m, d)
    h = _ffn(x, *wts["ffn1"])
    p = _inproj(h, wts["g_mix_pre"], wts["w_in"])
    shift_pad = jnp.pad(shift_prev.astype(F32), ((0, 0), (0, 0), (0, LORA_BLOCK - lora)))
    nh, hd = wkv_prev.shape[1], wkv_prev.shape[2]
    s0 = wkv_prev.astype(F32).transpose(0, 2, 1, 3).reshape(bsz, hd, nh * hd)
    mix_c, mix_r, new_conv, new_shift, s_new = _mixer(
        p.reshape(bsz, t, -1), conv_prev.astype(F32), shift_pad, s0, wts)
    new_wkv = s_new.reshape(bsz, hd, nh, hd).transpose(0, 2, 1, 3)
    h = _outproj(h, mix_c.reshape(m, -1), mix_r.reshape(m, -1), wts["w_out"], wts["g_mix_post"])
    out = _ffn(h, *wts["ffn2"])
    shift_dim = shift_prev.shape[-1]
    return out.reshape(bsz, t, d), new_conv, new_shift[:, :, :shift_dim], new_wkv


def kernel(x_prompt, x_sample, cache_conv, cache_shift, state_wkv, g_ffn1_pre, w_ffn1_gate, w_ffn1_up, w_ffn1_down, g_ffn1_post, g_mix_pre, w_in, conv_w, tshift_mu, w0, w_decay_up, a0, w_a_up, w_g_up, k_k, k_a, r_k, ln_x_w, ln_x_b, w_out, g_mix_post, g_ffn2_pre, w_ffn2_gate, w_ffn2_up, w_ffn2_down, g_ffn2_post):
    depth = w_in.shape[0]
    bp = x_prompt.shape[0]
    conv_rows, conv_ch = cache_conv.shape[2], cache_conv.shape[3]
    shift_dim = cache_shift.shape[-1]
    nh, hd = state_wkv.shape[2], state_wkv.shape[3]
    per_layer = (g_ffn1_pre, w_ffn1_gate, w_ffn1_up, w_ffn1_down, g_ffn1_post, g_mix_pre, w_in, conv_w,
                 tshift_mu, w0, w_decay_up, a0, w_a_up, w_g_up, k_k, k_a, r_k, ln_x_w, ln_x_b, w_out,
                 g_mix_post, g_ffn2_pre, w_ffn2_gate, w_ffn2_up, w_ffn2_down, g_ffn2_post)
    yp, ys = x_prompt, x_sample
    outs = [[] for _ in range(6)]
    for l in range(depth):
        wts = _prepare_weights(*(w[l] for w in per_layer))
        zc = jnp.zeros((bp, conv_rows, conv_ch), F32)
        zs = jnp.zeros((bp, 1, shift_dim), F32)
        zw = jnp.zeros((bp, nh, hd, hd), F32)
        yp, c1, s1, w1 = _layer(yp, zc, zs, zw, wts)
        ys, c2, s2, w2 = _layer(ys, cache_conv[l], cache_shift[l], state_wkv[l], wts)
        for lst, val, ref in zip(outs, (c1, s1, w1, c2, s2, w2),
                                 (cache_conv, cache_shift, state_wkv, cache_conv, cache_shift, state_wkv)):
            lst.append(val.astype(ref.dtype))
    return (yp, ys) + tuple(jnp.stack(o, 0) for o in outs)
```

```python
import functools

import jax
import jax.numpy as jnp
from jax import lax
from jax.experimental import pallas as pl
from jax.experimental.pallas import tpu as pltpu

F32 = jnp.float32
BF16 = jnp.bfloat16

RMS_EPS = 1e-6
GN_EPS = 64e-5
L2_EPS = 1e-12
HEAD_DIM = 64
WKV_CHUNK = 64
MXU_DIM = 256
WKV_GROUP = MXU_DIM // HEAD_DIM
PIPE_CHUNKS = 2
LANES = 128
LORA_BLOCK = 512
VMEM_LIMIT = 56 * 1024 * 1024
TILES = dict(ffn_tm=512, ffn_tf=512, inproj_tm=1024, inproj_tn=1024, mixer_tt=256, outproj_tm=512)
RESIDENT_ROWS = 1024


def _cparams(sem):
    return pltpu.CompilerParams(dimension_semantics=sem, vmem_limit_bytes=VMEM_LIMIT)


def _round_up(n, m):
    return (n + m - 1) // m * m


def _row_tile(m, pref):
    return pref if m % pref == 0 else m


def _rows_plan(m, pref):
    if m <= RESIDENT_ROWS or m % pref:
        return m, pl.Buffered(1)
    return pref, None


def _rms_scale(x):
    return lax.rsqrt(jnp.mean(x * x, axis=-1, keepdims=True) + RMS_EPS)


def _dot(a, b):
    return jnp.dot(a, b, preferred_element_type=F32)


def _dot_nt(a, b):
    return lax.dot_general(a, b, (((1,), (1,)), ((), ())), preferred_element_type=F32)


def _dot_tn(a, b):
    return lax.dot_general(a, b, (((0,), (0,)), ((), ())), preferred_element_type=F32)


def _split2(x):
    hi = x.astype(BF16)
    lo = (x - hi.astype(F32)).astype(BF16)
    return hi, lo


def _head_sum(x, bd):
    w = bd.shape[0]
    rows = x.shape[0]
    starts = range(0, x.shape[1], w)
    stacked = jnp.concatenate([x[:, s:s + w] for s in starts], axis=0).astype(BF16)
    sums = _dot(stacked, bd)
    return jnp.concatenate([sums[i * rows:(i + 1) * rows] for i in range(len(starts))], axis=1)


def _ffn_body(x_ref, gpre_ref, wg_ref, wu_ref, wd_ref, *rest, nb_main, has_tail):
    if has_tail:
        wgt_ref, wut_ref, wdt_ref, gpost_ref, o_ref, xn_ref, acc_ref = rest
    else:
        gpost_ref, o_ref, xn_ref, acc_ref = rest
    j = pl.program_id(1)

    @pl.when(j == 0)
    def _init():
        x = x_ref[...]
        xn_ref[...] = (x * _rms_scale(x) * gpre_ref[...]).astype(BF16)
        acc_ref[...] = jnp.zeros_like(acc_ref)

    def accumulate(wg, wu, wd):
        xn = xn_ref[...]
        gate = _dot(xn, wg[...])
        up = _dot(xn, wu[...])
        hmid = (gate * jax.nn.sigmoid(gate) * up).astype(BF16)
        acc_ref[...] += _dot(hmid, wd[...])

    if has_tail:
        pl.when(j < nb_main)(lambda: accumulate(wg_ref, wu_ref, wd_ref))
        pl.when(j == nb_main)(lambda: accumulate(wgt_ref, wut_ref, wdt_ref))
    else:
        accumulate(wg_ref, wu_ref, wd_ref)

    @pl.when(j == pl.num_programs(1) - 1)
    def _fin():
        a = acc_ref[...]
        o_ref[...] = x_ref[...] + 0.5 * (a * _rms_scale(a) * gpost_ref[...])


def _ffn(x, g_pre, wg, wu, wd, g_post):
    m, d = x.shape
    f = wg.shape[1]
    tm, rows_mode = _rows_plan(m, TILES["ffn_tm"])
    tf = min(TILES["ffn_tf"], f)
    nb_main = f // tf
    f_main = nb_main * tf
    assert (f - f_main) % LANES == 0
    has_tail = f_main < f
    last_main = nb_main - 1

    def whole(arr):
        return pl.BlockSpec(arr.shape, lambda i, j: (0, 0), pipeline_mode=pl.Buffered(1))

    tails = [wg[:, f_main:], wu[:, f_main:], wd[f_main:, :]] if has_tail else []
    return pl.pallas_call(
        functools.partial(_ffn_body, nb_main=nb_main, has_tail=has_tail),
        out_shape=jax.ShapeDtypeStruct((m, d), F32),
        grid=(m // tm, nb_main + has_tail),
        in_specs=[
            pl.BlockSpec((tm, d), lambda i, j: (i, 0), pipeline_mode=rows_mode),
            pl.BlockSpec((1, d), lambda i, j: (0, 0)),
            pl.BlockSpec((d, tf), lambda i, j: (0, jnp.minimum(j, last_main))),
            pl.BlockSpec((d, tf), lambda i, j: (0, jnp.minimum(j, last_main))),
            pl.BlockSpec((tf, d), lambda i, j: (jnp.minimum(j, last_main), 0)),
        ] + [whole(t) for t in tails] + [pl.BlockSpec((1, d), lambda i, j: (0, 0))],
        out_specs=pl.BlockSpec((tm, d), lambda i, j: (i, 0), pipeline_mode=rows_mode),
        scratch_shapes=[pltpu.VMEM((tm, d), BF16), pltpu.VMEM((tm, d), F32)],
        compiler_params=_cparams(("parallel", "arbitrary")),
        name="ffn",
    )(x, g_pre, wg, wu, wd, *tails, g_post)


def _inproj_body(x_ref, g_ref, w_ref, wl_ref, o_ref, ol_ref, xn_ref, *, nb_main):
    j = pl.program_id(1)

    @pl.when(j == 0)
    def _init():
        x = x_ref[...]
        xn_ref[...] = (x * _rms_scale(x) * g_ref[...]).astype(BF16)

    @pl.when(j < nb_main)
    def _main():
        o_ref[...] = _dot(xn_ref[...], w_ref[...])

    @pl.when(j == nb_main)
    def _lora():
        ol_ref[...] = _dot(xn_ref[...], wl_ref[...])


def _inproj(x, g, w, w_lora):
    m, d = x.shape
    n = w.shape[1]
    nl = w_lora.shape[1]
    tm, rows_mode = _rows_plan(m, TILES["inproj_tm"])
    tn = _row_tile(n, TILES["inproj_tn"])
    nb_main = n // tn
    last_main = nb_main - 1
    return pl.pallas_call(
        functools.partial(_inproj_body, nb_main=nb_main),
        out_shape=[jax.ShapeDtypeStruct((m, n), F32), jax.ShapeDtypeStruct((m, nl), F32)],
        grid=(m // tm, nb_main + 1),
        in_specs=[
            pl.BlockSpec((tm, d), lambda i, j: (i, 0), pipeline_mode=rows_mode),
            pl.BlockSpec((1, d), lambda i, j: (0, 0)),
            pl.BlockSpec((d, tn), lambda i, j: (0, jnp.minimum(j, last_main))),
            pl.BlockSpec((d, nl), lambda i, j: (0, 0), pipeline_mode=pl.Buffered(1)),
        ],
        out_specs=[pl.BlockSpec((tm, tn), lambda i, j: (i, jnp.minimum(j, last_main))),
                   pl.BlockSpec((tm, nl), lambda i, j: (i, 0))],
        scratch_shapes=[pltpu.VMEM((tm, d), BF16)],
        compiler_params=_cparams(("parallel", "arbitrary")),
        name="inproj",
    )(x, g, w, w_lora)


def _shift_rows(x, carry_rows):
    n = len(carry_rows)
    out = pltpu.roll(x, n, axis=0)
    row = lax.broadcasted_iota(jnp.int32, x.shape, 0)
    for i, c in enumerate(carry_rows):
        out = jnp.where(row == i, c, out)
    return out


def _bd(y, mask):
    reps = mask.shape[0] // y.shape[0]
    return jnp.where(mask, jnp.concatenate([y] * reps, axis=0), jnp.zeros((), y.dtype))


def _mm_heads(x, y, mask, nt=False):
    dot = _dot_nt if nt else _dot
    return dot(x.astype(BF16), _bd(y.astype(BF16), mask))


def _round_robin(tasks):
    while tasks:
        tasks = [t for t in tasks if next(t, tasks) is not tasks]


def _mixer_body(pb_ref, pc_ref, ph_ref, pr_ref, pk_ref, pv_ref, pl_ref, cprev_ref, sprev_ref, s0_ref,
                cw_ref, mu_ref, w0_ref, wdec_ref, a0_ref, wa_ref, wg_ref, kk_ref, ka_ref, rk_ref,
                lnw_ref, lnb_ref, bd_ref,
                mixc_ref, mixr_ref, nconv_ref, nshift_ref, sout_ref,
                cu_scr, cs_scr, s_scr, r_scr, lw_scr, k_scr, v_scr, kn_scr, b_scr, y_scr, bonus_scr, g_scr):
    ti = pl.program_id(1)
    c = WKV_CHUNK
    n = HEAD_DIM
    rd = pr_ref.shape[-1]
    nchunk = pc_ref.shape[1] // c
    wdt = WKV_GROUP * n
    ngroups = rd // wdt

    @pl.when(ti == 0)
    def _init():
        cu_scr[...] = cprev_ref[0]
        cs_scr[...] = sprev_ref[0]
        s_scr[...] = s0_ref[0]

    bd = bd_ref[...]
    row = lax.broadcasted_iota(jnp.int32, (c, c), 0)
    col = lax.broadcasted_iota(jnp.int32, (c, c), 1)
    ltri = jnp.where(row >= col, 1.0, 0.0).astype(BF16)
    prow = lax.broadcasted_iota(jnp.int32, (c, wdt), 0)
    pcol = lax.broadcasted_iota(jnp.int32, (c, wdt), 1) % n
    strict = prow > pcol
    incl = prow >= pcol
    brow = lax.broadcasted_iota(jnp.int32, (wdt, wdt), 0) // n
    bcol = lax.broadcasted_iota(jnp.int32, (wdt, wdt), 1) // n
    same_head = brow == bcol

    def prep_chunk(sl):
        u = pc_ref[0, sl, :] * ph_ref[0, sl, :]
        c0 = cu_scr[0:1, :]
        c1 = cu_scr[1:2, :]
        u2 = _shift_rows(u, [c0, c1])
        u1 = _shift_rows(u, [c1])
        conv = u2 * cw_ref[0:1, :] + u1 * cw_ref[1:2, :] + u * cw_ref[2:3, :]
        mixc_ref[0, sl, :] = (pb_ref[0, sl, :] * conv).astype(mixc_ref.dtype)
        nr = u.shape[0]
        cu_scr[...] = u[nr - 2:nr, :]
        yield

        def shifted(p, lo, hi):
            prev = _shift_rows(p, [cs_scr[0:1, lo:hi]])
            q = p + (prev - p) * mu_ref[0:1, lo:hi]
            cs_scr[0:1, lo:hi] = p[nr - 1:nr, :]
            return q

        ql = shifted(pl_ref[0, sl, :], 3 * rd, 3 * rd + LORA_BLOCK)
        ql_da = ql[:, 0:LANES]
        dec_in = w0_ref[...] + _dot(jnp.tanh(ql_da).astype(BF16), wdec_ref[...])
        a = jax.nn.sigmoid(a0_ref[...] + _dot(ql_da.astype(BF16), wa_ref[...]))
        g_scr[sl, :] = _dot(jax.nn.sigmoid(ql[:, LANES:LANES + wg_ref.shape[0]]).astype(BF16), wg_ref[...])
        yield
        z = -dec_in
        softplus = jnp.maximum(z, 0.0) + jnp.log(1.0 + jnp.exp(-jnp.abs(z)))
        w_log = -softplus - 0.5
        lw_scr[sl, :] = -jnp.exp(w_log)
        k = shifted(pk_ref[0, sl, :], rd, 2 * rd)
        kk = k * kk_ref[...]
        ssq = _head_sum(kk * kk, bd)
        yield
        kk = kk / jnp.maximum(jnp.sqrt(ssq), L2_EPS)
        kn_scr[sl, :] = kk
        b_scr[sl, :] = kk * a
        k = k * (1.0 + (a - 1.0) * ka_ref[...])
        k_scr[sl, :] = k
        r = shifted(pr_ref[0, sl, :], 0, rd)
        r_scr[sl, :] = r
        rk_sum = _head_sum(r * k * rk_ref[...], bd)
        yield
        v = shifted(pv_ref[0, sl, :], 2 * rd, 3 * rd)
        v_scr[sl, :] = v
        bonus_scr[sl, :] = rk_sum * v

    def wkv_pre(sl, ln, out):
        lw = lw_scr[sl, ln]
        l_hi, l_lo = _split2(lw)
        cum = _dot(ltri, jnp.concatenate([l_hi, l_lo], axis=1))
        cum = cum[:, :wdt] + cum[:, wdt:]
        yield
        cum_last = cum[c - 1:c, :]
        g_end = jnp.exp(cum_last - cum)
        g_inv = jnp.exp(-cum)
        kk = k_scr[sl, ln]
        vv = v_scr[sl, ln]
        bb = b_scr[sl, ln]
        ap = jnp.concatenate([kn_scr[sl, ln] * jnp.exp(cum - lw), r_scr[sl, ln] * jnp.exp(cum)], axis=0)
        g_b = _mm_heads(ap, bb * g_inv, same_head, nt=True)
        g_k = _mm_heads(ap, kk * g_inv, same_head, nt=True)
        yield
        q = jnp.where(strict, -g_b[:c], 0.0)
        lk = jnp.concatenate([jnp.where(strict, g_k[:c], 0.0), jnp.where(incl, g_k[c:], 0.0)], axis=0)
        t1v = _mm_heads(lk, vv, same_head)
        z = q
        q = _mm_heads(q, q, same_head)
        yield
        for level in range(5):
            if level < 4:
                zq_qq = _mm_heads(jnp.concatenate([z, q], axis=0), q, same_head)
                z, q = z + q + zq_qq[:c], zq_qq[c:]
            else:
                z = z + q + _mm_heads(z, q, same_head)
            yield
        out.update(ap=ap.astype(BF16), t1v=t1v, z=z.astype(BF16), irb=jnp.where(incl, g_b[c:], 0.0).astype(BF16),
                   vv=vv, kb=jnp.concatenate([kk * g_end, bb * g_end], axis=0).astype(BF16),
                   g_last=jnp.exp(cum_last))

    def wkv_post(sl, ln, pre):
        s0 = s_scr[:, ln]
        a_s = _dot_nt(pre["ap"], _bd(s0.astype(BF16), same_head))
        yield
        t1 = a_s + pre["t1v"]
        u = t1[:c] + _dot(pre["z"], _bd(t1[:c].astype(BF16), same_head))
        yield
        ub = u.astype(BF16)
        y_scr[sl, ln] = t1[c:] - _dot(pre["irb"], _bd(ub, same_head))
        vu = jnp.concatenate([pre["vv"].astype(BF16), -ub], axis=0)
        gram = jnp.where(same_head, _dot_tn(vu, pre["kb"]), 0.0)
        upd = gram[0:n]
        for h in range(1, WKV_GROUP):
            upd = upd + gram[h * n:(h + 1) * n]
        s_scr[:, ln] = s0 * pre["g_last"] + upd

    def out_chunk(sl):
        y = y_scr[sl, :]
        inv_n = 1.0 / HEAD_DIM
        mean = _head_sum(y, bd) * inv_n
        yield
        yc = y - mean
        var = _head_sum(yc * yc, bd) * inv_n
        yield
        yn = yc * lax.rsqrt(var + GN_EPS) * lnw_ref[...] + lnb_ref[...]
        mixr_ref[0, sl, :] = ((yn + bonus_scr[sl, :]) * g_scr[sl, :]).astype(mixr_ref.dtype)

    def rows(i):
        return slice(i * c, (i + 1) * c)

    def in_turn(gens):
        for gen in gens:
            yield from gen

    lanes = [slice(g * wdt, (g + 1) * wdt) for g in range(ngroups)]
    pre = [[{} for _ in lanes] for _ in range(nchunk)]
    span = PIPE_CHUNKS if nchunk % PIPE_CHUNKS == 0 else 1
    nstage = nchunk // span

    def chunks_of(stage):
        return range(stage * span, (stage + 1) * span) if 0 <= stage < nstage else ()

    for step in range(nstage + 3):
        tasks = [in_turn([wkv_post(rows(i), ln, pre[i][g]) for i in chunks_of(step - 2)])
                 for g, ln in enumerate(lanes)]
        tasks += [wkv_pre(rows(i), ln, pre[i][g]) for i in chunks_of(step - 1) for g, ln in enumerate(lanes)]
        tasks.append(in_turn([prep_chunk(rows(i)) for i in chunks_of(step)]))
        tasks.append(in_turn([out_chunk(rows(i)) for i in chunks_of(step - 3)]))
        _round_robin(tasks)

    @pl.when(ti == pl.num_programs(1) - 1)
    def _fin():
        nconv_ref[0] = cu_scr[...]
        nshift_ref[0] = cs_scr[...]
        sout_ref[0] = s_scr[...]


def _mixer(p3, pl3, conv_prev, shift_prev, s0, wts):
    bsz, t, _ = p3.shape
    cc = conv_prev.shape[-1]
    rd = wts["w0"].shape[-1]
    sp = shift_prev.shape[-1]
    tt = _row_tile(t, TILES["mixer_tt"])
    assert tt % WKV_CHUNK == 0 and rd % (WKV_GROUP * HEAD_DIM) == 0 and cc == rd

    def col(width, idx):
        return pl.BlockSpec((1, tt, width), lambda b, i: (b, i, idx))

    def const(arr):
        return pl.BlockSpec(arr.shape, lambda b, i: (0,) * arr.ndim)

    def per_stream(rows, width):
        return pl.BlockSpec((1, rows, width), lambda b, i: (b, 0, 0))

    names = ("conv_w", "mu", "w0", "wdec", "a0", "wa", "wg", "k_k", "k_a", "r_k", "ln_w", "ln_b", "bd")
    consts = [wts[n] for n in names]
    tok = pltpu.VMEM((tt, rd), F32)
    return pl.pallas_call(
        _mixer_body,
        out_shape=[jax.ShapeDtypeStruct((bsz, t, cc), BF16), jax.ShapeDtypeStruct((bsz, t, rd), BF16),
                   jax.ShapeDtypeStruct((bsz, 2, cc), F32), jax.ShapeDtypeStruct((bsz, 1, sp), F32),
                   jax.ShapeDtypeStruct((bsz, HEAD_DIM, rd), F32)],
        grid=(bsz, t // tt),
        in_specs=[col(cc, 0), col(cc, 1), col(cc, 2), col(rd, 3), col(rd, 4), col(rd, 5),
                  col(LORA_BLOCK, 0),
                  per_stream(2, cc), per_stream(1, sp), per_stream(HEAD_DIM, rd)]
        + [const(c) for c in consts],
        out_specs=[pl.BlockSpec((1, tt, cc), lambda b, i: (b, i, 0)), pl.BlockSpec((1, tt, rd), lambda b, i: (b, i, 0)),
                   per_stream(2, cc), per_stream(1, sp), per_stream(HEAD_DIM, rd)],
        scratch_shapes=[pltpu.VMEM((2, cc), F32), pltpu.VMEM((1, sp), F32), pltpu.VMEM((HEAD_DIM, rd), F32)]
        + [tok] * 9,
        compiler_params=_cparams(("parallel", "arbitrary")),
        name="mixer",
    )(p3, p3, p3, p3, p3, p3, pl3, conv_prev, shift_prev, s0, *consts)


def _outproj_body(h_ref, mc_ref, mr_ref, wo_ref, gpost_ref, o_ref):
    cc = mc_ref.shape[-1]
    m = _dot(mc_ref[...], wo_ref[0:cc, :]) + _dot(mr_ref[...], wo_ref[cc:, :])
    o_ref[...] = h_ref[...] + m * _rms_scale(m) * gpost_ref[...]


def _outproj(h, mix_c, mix_r, wo, g_post):
    m, d = h.shape
    tm = _row_tile(m, TILES["outproj_tm"])

    def rows(width):
        return pl.BlockSpec((tm, width), lambda i: (i, 0))

    return pl.pallas_call(
        _outproj_body,
        out_shape=jax.ShapeDtypeStruct((m, d), F32),
        grid=(m // tm,),
        in_specs=[rows(d), rows(mix_c.shape[1]), rows(mix_r.shape[1]),
                  pl.BlockSpec(wo.shape, lambda i: (0, 0), pipeline_mode=pl.Buffered(1)),
                  pl.BlockSpec(g_post.shape, lambda i: (0, 0))],
        out_specs=rows(d),
        compiler_params=_cparams(("parallel",)),
        name="outproj",
    )(h, mix_c, mix_r, wo, g_post)


def _prepare_weights(g_ffn1_pre, w_ffn1_gate, w_ffn1_up, w_ffn1_down, g_ffn1_post, g_mix_pre, w_in, conv_w,
                     tshift_mu, w0, w_decay_up, a0, w_a_up, w_g_up, k_k, k_a, r_k, ln_x_w, ln_x_b, w_out,
                     g_mix_post, g_ffn2_pre, w_ffn2_gate, w_ffn2_up, w_ffn2_down, g_ffn2_post):
    rd = w0.shape[-1]
    dl, al, gl = w_decay_up.shape[0], w_a_up.shape[0], w_g_up.shape[0]
    assert dl + al == LANES and gl <= LORA_BLOCK - LANES and rd % HEAD_DIM == 0
    lora = dl + al + gl
    gl_pad = _round_up(gl, LANES)

    def row(vec):
        return vec.reshape(1, -1).astype(F32)

    def ffn_w(wg, wu, wd):
        return wg.astype(BF16), wu.astype(BF16), wd.astype(BF16)

    head = jnp.arange(MXU_DIM, dtype=jnp.int32) // HEAD_DIM
    return dict(
        ffn1=(row(g_ffn1_pre),) + ffn_w(w_ffn1_gate, w_ffn1_up, w_ffn1_down) + (row(g_ffn1_post),),
        ffn2=(row(g_ffn2_pre),) + ffn_w(w_ffn2_gate, w_ffn2_up, w_ffn2_down) + (row(g_ffn2_post),),
        g_mix_pre=row(g_mix_pre),
        w_in=w_in[:, :w_in.shape[1] - lora].astype(BF16),
        w_in_lora=jnp.pad(w_in[:, w_in.shape[1] - lora:].astype(BF16), ((0, 0), (0, LORA_BLOCK - lora))),
        lora=lora,
        conv_w=conv_w.astype(F32),
        mu=jnp.pad(row(tshift_mu), ((0, 0), (0, LORA_BLOCK - lora))),
        w0=row(w0),
        wdec=jnp.pad(w_decay_up.astype(BF16), ((0, al), (0, 0))),
        a0=row(a0),
        wa=jnp.pad(w_a_up.astype(BF16), ((dl, 0), (0, 0))),
        wg=jnp.pad(w_g_up.astype(BF16), ((0, gl_pad - gl), (0, 0))),
        k_k=row(k_k), k_a=row(k_a), r_k=row(r_k),
        bd=(head[:, None] == head[None, :]).astype(BF16),
        ln_w=row(ln_x_w), ln_b=row(ln_x_b),
        w_out=w_out.astype(BF16),
        g_mix_post=row(g_mix_post),
    )


def _layer(x3, conv_prev, shift_prev, wkv_prev, wts):
    bsz, t, d = x3.shape
    m = bsz * t
    lora = wts["lora"]
    x = x3.reshape(m, d)
    h = _ffn(x, *wts["ffn1"])
    p, p_lora = _inproj(h, wts["g_mix_pre"], wts["w_in"], wts["w_in_lora"])
    shift_pad = jnp.pad(shift_prev.astype(F32), ((0, 0), (0, 0), (0, LORA_BLOCK - lora)))
    nh, hd = wkv_prev.shape[1], wkv_prev.shape[2]
    s0 = wkv_prev.astype(F32).transpose(0, 2, 1, 3).reshape(bsz, hd, nh * hd)
    mix_c, mix_r, new_conv, new_shift, s_new = _mixer(
        p.reshape(bsz, t, -1), p_lora.reshape(bsz, t, -1), conv_prev.astype(F32), shift_pad, s0, wts)
    new_wkv = s_new.reshape(bsz, hd, nh, hd).transpose(0, 2, 1, 3)
    h = _outproj(h, mix_c.reshape(m, -1), mix_r.reshape(m, -1), wts["w_out"], wts["g_mix_post"])
    out = _ffn(h, *wts["ffn2"])
    shift_dim = shift_prev.shape[-1]
    return out.reshape(bsz, t, d), new_conv, new_shift[:, :, :shift_dim], new_wkv


def kernel(x_prompt, x_sample, cache_conv, cache_shift, state_wkv, g_ffn1_pre, w_ffn1_gate, w_ffn1_up, w_ffn1_down, g_ffn1_post, g_mix_pre, w_in, conv_w, tshift_mu, w0, w_decay_up, a0, w_a_up, w_g_up, k_k, k_a, r_k, ln_x_w, ln_x_b, w_out, g_mix_post, g_ffn2_pre, w_ffn2_gate, w_ffn2_up, w_ffn2_down, g_ffn2_post):
    depth = w_in.shape[0]
    bp = x_prompt.shape[0]
    conv_rows, conv_ch = cache_conv.shape[2], cache_conv.shape[3]
    shift_dim = cache_shift.shape[-1]
    nh, hd = state_wkv.shape[2], state_wkv.shape[3]
    per_layer = (g_ffn1_pre, w_ffn1_gate, w_ffn1_up, w_ffn1_down, g_ffn1_post, g_mix_pre, w_in, conv_w,
                 tshift_mu, w0, w_decay_up, a0, w_a_up, w_g_up, k_k, k_a, r_k, ln_x_w, ln_x_b, w_out,
                 g_mix_post, g_ffn2_pre, w_ffn2_gate, w_ffn2_up, w_ffn2_down, g_ffn2_post)
    yp, ys = x_prompt, x_sample
    outs = [[] for _ in range(6)]
    for l in range(depth):
        wts = _prepare_weights(*(w[l] for w in per_layer))
        zc = jnp.zeros((bp, conv_rows, conv_ch), F32)
        zs = jnp.zeros((bp, 1, shift_dim), F32)
        zw = jnp.zeros((bp, nh, hd, hd), F32)
        yp, c1, s1, w1 = _layer(yp, zc, zs, zw, wts)
        ys, c2, s2, w2 = _layer(ys, cache_conv[l], cache_shift[l], state_wkv[l], wts)
        for lst, val, ref in zip(outs, (c1, s1, w1, c2, s2, w2),
                                 (cache_conv, cache_shift, state_wkv, cache_conv, cache_shift, state_wkv)):
            lst.append(val.astype(ref.dtype))
    return (yp, ys) + tuple(jnp.stack(o, 0) for o in outs)
```

```python
import functools

import jax
import jax.numpy as jnp
from jax import lax
from jax.experimental import pallas as pl
from jax.experimental.pallas import tpu as pltpu

F32 = jnp.float32
BF16 = jnp.bfloat16

RMS_EPS = 1e-6
GN_EPS = 64e-5
L2_EPS = 1e-12
HEAD_DIM = 64
WKV_CHUNK = 64
MXU_DIM = 256
WKV_GROUP = MXU_DIM // HEAD_DIM
PIPE_CHUNKS = 2
LANES = 128
LORA_BLOCK = 512
VMEM_LIMIT = 56 * 1024 * 1024
TILES = dict(ffn_tm=512, ffn_tf=512, inproj_tm=1024, inproj_tn=1664, mixer_tt=256, outproj_tm=512)
RESIDENT_ROWS = 1024


def _cparams(sem):
    return pltpu.CompilerParams(dimension_semantics=sem, vmem_limit_bytes=VMEM_LIMIT)


def _round_up(n, m):
    return (n + m - 1) // m * m


def _row_tile(m, pref):
    return pref if m % pref == 0 else m


def _rows_plan(m, pref):
    if m <= RESIDENT_ROWS or m % pref:
        return m, pl.Buffered(1)
    return pref, None


def _rms_scale(x):
    return lax.rsqrt(jnp.mean(x * x, axis=-1, keepdims=True) + RMS_EPS)


def _dot(a, b):
    return jnp.dot(a, b, preferred_element_type=F32)


def _dot_nt(a, b):
    return lax.dot_general(a, b, (((1,), (1,)), ((), ())), preferred_element_type=F32)


def _dot_tn(a, b):
    return lax.dot_general(a, b, (((0,), (0,)), ((), ())), preferred_element_type=F32)


def _split2(x):
    hi = x.astype(BF16)
    lo = (x - hi.astype(F32)).astype(BF16)
    return hi, lo


def _head_sum(x, bd):
    w = bd.shape[0]
    rows = x.shape[0]
    starts = range(0, x.shape[1], w)
    stacked = jnp.concatenate([x[:, s:s + w] for s in starts], axis=0).astype(BF16)
    sums = _dot(stacked, bd)
    return jnp.concatenate([sums[i * rows:(i + 1) * rows] for i in range(len(starts))], axis=1)


def _ffn_body(x_ref, gpre_ref, wg_ref, wu_ref, wd_ref, *rest, nb_main, has_tail):
    if has_tail:
        wgt_ref, wut_ref, wdt_ref, gpost_ref, o_ref, xn_ref, acc_ref = rest
    else:
        gpost_ref, o_ref, xn_ref, acc_ref = rest
    j = pl.program_id(1)

    @pl.when(j == 0)
    def _init():
        x = x_ref[...]
        xn_ref[...] = (x * _rms_scale(x) * gpre_ref[...]).astype(BF16)
        acc_ref[...] = jnp.zeros_like(acc_ref)

    def accumulate(wg, wu, wd):
        xn = xn_ref[...]
        gate = _dot(xn, wg[...])
        up = _dot(xn, wu[...])
        hmid = (gate * jax.nn.sigmoid(gate) * up).astype(BF16)
        acc_ref[...] += _dot(hmid, wd[...])

    if has_tail:
        pl.when(j < nb_main)(lambda: accumulate(wg_ref, wu_ref, wd_ref))
        pl.when(j == nb_main)(lambda: accumulate(wgt_ref, wut_ref, wdt_ref))
    else:
        accumulate(wg_ref, wu_ref, wd_ref)

    @pl.when(j == pl.num_programs(1) - 1)
    def _fin():
        a = acc_ref[...]
        o_ref[...] = x_ref[...] + 0.5 * (a * _rms_scale(a) * gpost_ref[...])


def _ffn(x, g_pre, wg, wu, wd, g_post):
    m, d = x.shape
    f = wg.shape[1]
    tm, rows_mode = _rows_plan(m, TILES["ffn_tm"])
    tf = min(TILES["ffn_tf"], f)
    nb_main = f // tf
    f_main = nb_main * tf
    assert (f - f_main) % LANES == 0
    has_tail = f_main < f
    last_main = nb_main - 1

    def whole(arr):
        return pl.BlockSpec(arr.shape, lambda i, j: (0, 0), pipeline_mode=pl.Buffered(1))

    tails = [wg[:, f_main:], wu[:, f_main:], wd[f_main:, :]] if has_tail else []
    return pl.pallas_call(
        functools.partial(_ffn_body, nb_main=nb_main, has_tail=has_tail),
        out_shape=jax.ShapeDtypeStruct((m, d), F32),
        grid=(m // tm, nb_main + has_tail),
        in_specs=[
            pl.BlockSpec((tm, d), lambda i, j: (i, 0), pipeline_mode=rows_mode),
            pl.BlockSpec((1, d), lambda i, j: (0, 0)),
            pl.BlockSpec((d, tf), lambda i, j: (0, jnp.minimum(j, last_main))),
            pl.BlockSpec((d, tf), lambda i, j: (0, jnp.minimum(j, last_main))),
            pl.BlockSpec((tf, d), lambda i, j: (jnp.minimum(j, last_main), 0)),
        ] + [whole(t) for t in tails] + [pl.BlockSpec((1, d), lambda i, j: (0, 0))],
        out_specs=pl.BlockSpec((tm, d), lambda i, j: (i, 0), pipeline_mode=rows_mode),
        scratch_shapes=[pltpu.VMEM((tm, d), BF16), pltpu.VMEM((tm, d), F32)],
        compiler_params=_cparams(("parallel", "arbitrary")),
        name="ffn",
    )(x, g_pre, wg, wu, wd, *tails, g_post)


def _inproj_body(x_ref, g_ref, w_ref, o_ref, xn_ref):
    @pl.when(pl.program_id(1) == 0)
    def _init():
        x = x_ref[...]
        xn_ref[...] = (x * _rms_scale(x) * g_ref[...]).astype(BF16)

    o_ref[...] = _dot(xn_ref[...], w_ref[...])


def _inproj(x, g, w):
    m, d = x.shape
    n = w.shape[1]
    tm, rows_mode = _rows_plan(m, TILES["inproj_tm"])
    tn = _row_tile(n, TILES["inproj_tn"])
    return pl.pallas_call(
        _inproj_body,
        out_shape=jax.ShapeDtypeStruct((m, n), F32),
        grid=(m // tm, n // tn),
        in_specs=[
            pl.BlockSpec((tm, d), lambda i, j: (i, 0), pipeline_mode=rows_mode),
            pl.BlockSpec((1, d), lambda i, j: (0, 0)),
            pl.BlockSpec((d, tn), lambda i, j: (0, j)),
        ],
        out_specs=pl.BlockSpec((tm, tn), lambda i, j: (i, j)),
        scratch_shapes=[pltpu.VMEM((tm, d), BF16)],
        compiler_params=_cparams(("parallel", "arbitrary")),
        name="inproj",
    )(x, g, w)


def _shift_rows(x, carry_rows):
    n = len(carry_rows)
    out = pltpu.roll(x, n, axis=0)
    row = lax.broadcasted_iota(jnp.int32, x.shape, 0)
    for i, c in enumerate(carry_rows):
        out = jnp.where(row == i, c, out)
    return out


def _bd(y, mask):
    reps = mask.shape[0] // y.shape[0]
    return jnp.where(mask, jnp.concatenate([y] * reps, axis=0), jnp.zeros((), y.dtype))


def _mm_heads(x, y, mask, nt=False):
    dot = _dot_nt if nt else _dot
    return dot(x.astype(BF16), _bd(y.astype(BF16), mask))


def _cast_block(src_ref, dst_ref):
    dst_ref[...] = src_ref[...].astype(dst_ref.dtype)


def _side_blocks(rows, steps):
    return max(n for n in range(1, steps + 1) if rows % n == 0 and (rows // n) % 16 == 0)


def _round_robin(tasks):
    while tasks:
        tasks = [t for t in tasks if next(t, tasks) is not tasks]


def _mixer_body(*refs, nside, side_blocks):
    n_in = 23
    (pb_ref, pc_ref, ph_ref, pr_ref, pk_ref, pv_ref, pl_ref, cprev_ref, sprev_ref, s0_ref,
     cw_ref, mu_ref, w0_ref, wdec_ref, a0_ref, wa_ref, wg_ref, kk_ref, ka_ref, rk_ref,
     lnw_ref, lnb_ref, bd_ref) = refs[:n_in]
    side_in = refs[n_in:n_in + nside]
    mixc_ref, mixr_ref, nconv_ref, nshift_ref, sout_ref = refs[n_in + nside:n_in + nside + 5]
    side_out = refs[n_in + nside + 5:n_in + 2 * nside + 5]
    (cu_scr, cs_scr, s_scr, r_scr, lw_scr, k_scr, v_scr, kn_scr, b_scr, y_scr, bonus_scr,
     g_scr) = refs[n_in + 2 * nside + 5:]
    ti = pl.program_id(1)
    step_no = pl.program_id(0) * pl.num_programs(1) + ti
    for src, dst, nblk in zip(side_in, side_out, side_blocks):
        pl.when(step_no < nblk)(functools.partial(_cast_block, src, dst))
    c = WKV_CHUNK
    n = HEAD_DIM
    rd = pr_ref.shape[-1]
    nchunk = pc_ref.shape[1] // c
    wdt = WKV_GROUP * n
    ngroups = rd // wdt

    @pl.when(ti == 0)
    def _init():
        cu_scr[...] = cprev_ref[0]
        cs_scr[...] = sprev_ref[0]
        s_scr[...] = s0_ref[0]

    bd = bd_ref[...]
    row = lax.broadcasted_iota(jnp.int32, (c, c), 0)
    col = lax.broadcasted_iota(jnp.int32, (c, c), 1)
    ltri = jnp.where(row >= col, 1.0, 0.0).astype(BF16)
    prow = lax.broadcasted_iota(jnp.int32, (c, wdt), 0)
    pcol = lax.broadcasted_iota(jnp.int32, (c, wdt), 1) % n
    strict = prow > pcol
    incl = prow >= pcol
    brow = lax.broadcasted_iota(jnp.int32, (wdt, wdt), 0) // n
    bcol = lax.broadcasted_iota(jnp.int32, (wdt, wdt), 1) // n
    same_head = brow == bcol

    def prep_chunk(sl):
        u = pc_ref[0, sl, :] * ph_ref[0, sl, :]
        c0 = cu_scr[0:1, :]
        c1 = cu_scr[1:2, :]
        u2 = _shift_rows(u, [c0, c1])
        u1 = _shift_rows(u, [c1])
        conv = u2 * cw_ref[0:1, :] + u1 * cw_ref[1:2, :] + u * cw_ref[2:3, :]
        mixc_ref[0, sl, :] = (pb_ref[0, sl, :] * conv).astype(mixc_ref.dtype)
        nr = u.shape[0]
        cu_scr[...] = u[nr - 2:nr, :]
        yield

        def shifted(p, lo, hi):
            prev = _shift_rows(p, [cs_scr[0:1, lo:hi]])
            q = p + (prev - p) * mu_ref[0:1, lo:hi]
            cs_scr[0:1, lo:hi] = p[nr - 1:nr, :]
            return q

        ql = shifted(pl_ref[0, sl, :], 3 * rd, 3 * rd + LORA_BLOCK)
        ql_da = ql[:, 0:LANES]
        dec_in = w0_ref[...] + _dot(jnp.tanh(ql_da).astype(BF16), wdec_ref[...])
        a = jax.nn.sigmoid(a0_ref[...] + _dot(ql_da.astype(BF16), wa_ref[...]))
        g_scr[sl, :] = _dot(jax.nn.sigmoid(ql[:, LANES:LANES + wg_ref.shape[0]]).astype(BF16), wg_ref[...])
        yield
        z = -dec_in
        softplus = jnp.maximum(z, 0.0) + jnp.log(1.0 + jnp.exp(-jnp.abs(z)))
        w_log = -softplus - 0.5
        lw_scr[sl, :] = -jnp.exp(w_log)
        k = shifted(pk_ref[0, sl, :], rd, 2 * rd)
        kk = k * kk_ref[...]
        ssq = _head_sum(kk * kk, bd)
        yield
        kk = kk / jnp.maximum(jnp.sqrt(ssq), L2_EPS)
        kn_scr[sl, :] = kk
        b_scr[sl, :] = kk * a
        k = k * (1.0 + (a - 1.0) * ka_ref[...])
        k_scr[sl, :] = k
        r = shifted(pr_ref[0, sl, :], 0, rd)
        r_scr[sl, :] = r
        rk_sum = _head_sum(r * k * rk_ref[...], bd)
        yield
        v = shifted(pv_ref[0, sl, :], 2 * rd, 3 * rd)
        v_scr[sl, :] = v
        bonus_scr[sl, :] = rk_sum * v

    def wkv_pre(sl, ln, out):
        lw = lw_scr[sl, ln]
        l_hi, l_lo = _split2(lw)
        cum = _dot(ltri, jnp.concatenate([l_hi, l_lo], axis=1))
        cum = cum[:, :wdt] + cum[:, wdt:]
        yield
        cum_last = cum[c - 1:c, :]
        g_end = jnp.exp(cum_last - cum)
        g_inv = jnp.exp(-cum)
        kk = k_scr[sl, ln]
        vv = v_scr[sl, ln]
        bb = b_scr[sl, ln]
        ap = jnp.concatenate([kn_scr[sl, ln] * jnp.exp(cum - lw), r_scr[sl, ln] * jnp.exp(cum)], axis=0)
        g_b = _mm_heads(ap, bb * g_inv, same_head, nt=True)
        g_k = _mm_heads(ap, kk * g_inv, same_head, nt=True)
        yield
        q = jnp.where(strict, -g_b[:c], 0.0)
        lk = jnp.concatenate([jnp.where(strict, g_k[:c], 0.0), jnp.where(incl, g_k[c:], 0.0)], axis=0)
        t1v = _mm_heads(lk, vv, same_head)
        z = q
        q = _mm_heads(q, q, same_head)
        yield
        for level in range(5):
            if level < 4:
                zq_qq = _mm_heads(jnp.concatenate([z, q], axis=0), q, same_head)
                z, q = z + q + zq_qq[:c], zq_qq[c:]
            else:
                z = z + q + _mm_heads(z, q, same_head)
            yield
        out.update(ap=ap.astype(BF16), t1v=t1v, z=z.astype(BF16), irb=jnp.where(incl, g_b[c:], 0.0).astype(BF16),
                   vv=vv, kb=jnp.concatenate([kk * g_end, bb * g_end], axis=0).astype(BF16),
                   g_last=jnp.exp(cum_last))

    def wkv_post(sl, ln, pre):
        s0 = s_scr[:, ln]
        a_s = _dot_nt(pre["ap"], _bd(s0.astype(BF16), same_head))
        yield
        t1 = a_s + pre["t1v"]
        u = t1[:c] + _dot(pre["z"], _bd(t1[:c].astype(BF16), same_head))
        yield
        ub = u.astype(BF16)
        y_scr[sl, ln] = t1[c:] - _dot(pre["irb"], _bd(ub, same_head))
        vu = jnp.concatenate([pre["vv"].astype(BF16), -ub], axis=0)
        gram = jnp.where(same_head, _dot_tn(vu, pre["kb"]), 0.0)
        upd = gram[0:n]
        for h in range(1, WKV_GROUP):
            upd = upd + gram[h * n:(h + 1) * n]
        s_scr[:, ln] = s0 * pre["g_last"] + upd

    def out_chunk(sl):
        y = y_scr[sl, :]
        inv_n = 1.0 / HEAD_DIM
        mean = _head_sum(y, bd) * inv_n
        yield
        yc = y - mean
        var = _head_sum(yc * yc, bd) * inv_n
        yield
        yn = yc * lax.rsqrt(var + GN_EPS) * lnw_ref[...] + lnb_ref[...]
        mixr_ref[0, sl, :] = ((yn + bonus_scr[sl, :]) * g_scr[sl, :]).astype(mixr_ref.dtype)

    def rows(i):
        return slice(i * c, (i + 1) * c)

    def in_turn(gens):
        for gen in gens:
            yield from gen

    lanes = [slice(g * wdt, (g + 1) * wdt) for g in range(ngroups)]
    pre = [[{} for _ in lanes] for _ in range(nchunk)]
    span = PIPE_CHUNKS if nchunk % PIPE_CHUNKS == 0 else 1
    nstage = nchunk // span

    def chunks_of(stage):
        return range(stage * span, (stage + 1) * span) if 0 <= stage < nstage else ()

    for step in range(nstage + 3):
        tasks = [in_turn([wkv_post(rows(i), ln, pre[i][g]) for i in chunks_of(step - 2)])
                 for g, ln in enumerate(lanes)]
        tasks += [wkv_pre(rows(i), ln, pre[i][g]) for i in chunks_of(step - 1) for g, ln in enumerate(lanes)]
        tasks.append(in_turn([prep_chunk(rows(i)) for i in chunks_of(step)]))
        tasks.append(in_turn([out_chunk(rows(i)) for i in chunks_of(step - 3)]))
        _round_robin(tasks)

    @pl.when(ti == pl.num_programs(1) - 1)
    def _fin():
        nconv_ref[0] = cu_scr[...]
        nshift_ref[0] = cs_scr[...]
        sout_ref[0] = s_scr[...]


def _mixer(p3, conv_prev, shift_prev, s0, wts, side=()):
    bsz, t, _ = p3.shape
    cc = conv_prev.shape[-1]
    rd = wts["w0"].shape[-1]
    sp = shift_prev.shape[-1]
    tt = _row_tile(t, TILES["mixer_tt"])
    assert tt % WKV_CHUNK == 0 and rd % (WKV_GROUP * HEAD_DIM) == 0
    lora_blk = (3 * cc + 3 * rd) // LORA_BLOCK
    nt = t // tt

    def col(width, idx):
        return pl.BlockSpec((1, tt, width), lambda b, i: (b, i, idx))

    def const(arr):
        return pl.BlockSpec(arr.shape, lambda b, i: (0,) * arr.ndim)

    def per_stream(rows, width):
        return pl.BlockSpec((1, rows, width), lambda b, i: (b, 0, 0))

    side_blocks = tuple(_side_blocks(w.shape[0], bsz * nt) for w in side)

    def side_spec(w, nblk):
        return pl.BlockSpec((w.shape[0] // nblk, w.shape[1]), lambda b, i: (jnp.minimum(b * nt + i, nblk - 1), 0))

    side_specs = [side_spec(w, nblk) for w, nblk in zip(side, side_blocks)]
    names = ("conv_w", "mu", "w0", "wdec", "a0", "wa", "wg", "k_k", "k_a", "r_k", "ln_w", "ln_b", "bd")
    consts = [wts[n] for n in names]
    tok = pltpu.VMEM((tt, rd), F32)
    return pl.pallas_call(
        functools.partial(_mixer_body, nside=len(side), side_blocks=side_blocks),
        out_shape=[jax.ShapeDtypeStruct((bsz, t, cc), BF16), jax.ShapeDtypeStruct((bsz, t, rd), BF16),
                   jax.ShapeDtypeStruct((bsz, 2, cc), F32), jax.ShapeDtypeStruct((bsz, 1, sp), F32),
                   jax.ShapeDtypeStruct((bsz, HEAD_DIM, rd), F32)]
        + [jax.ShapeDtypeStruct(w.shape, BF16) for w in side],
        grid=(bsz, nt),
        in_specs=[col(cc, 0), col(cc, 1), col(cc, 2), col(rd, 3), col(rd, 4), col(rd, 5),
                  col(LORA_BLOCK, lora_blk),
                  per_stream(2, cc), per_stream(1, sp), per_stream(HEAD_DIM, rd)]
        + [const(c) for c in consts] + side_specs,
        out_specs=[pl.BlockSpec((1, tt, cc), lambda b, i: (b, i, 0)), pl.BlockSpec((1, tt, rd), lambda b, i: (b, i, 0)),
                   per_stream(2, cc), per_stream(1, sp), per_stream(HEAD_DIM, rd)] + side_specs,
        scratch_shapes=[pltpu.VMEM((2, cc), F32), pltpu.VMEM((1, sp), F32), pltpu.VMEM((HEAD_DIM, rd), F32)]
        + [tok] * 9,
        compiler_params=_cparams(("arbitrary", "arbitrary")),
        name="mixer",
    )(p3, p3, p3, p3, p3, p3, p3, conv_prev, shift_prev, s0, *consts, *side)


def _outproj_body(h_ref, mc_ref, mr_ref, wo_ref, gpost_ref, o_ref):
    cc = mc_ref.shape[-1]
    m = _dot(mc_ref[...], wo_ref[0:cc, :]) + _dot(mr_ref[...], wo_ref[cc:, :])
    o_ref[...] = h_ref[...] + m * _rms_scale(m) * gpost_ref[...]


def _outproj(h, mix_c, mix_r, wo, g_post):
    m, d = h.shape
    tm = _row_tile(m, TILES["outproj_tm"])

    def rows(width):
        return pl.BlockSpec((tm, width), lambda i: (i, 0))

    return pl.pallas_call(
        _outproj_body,
        out_shape=jax.ShapeDtypeStruct((m, d), F32),
        grid=(m // tm,),
        in_specs=[rows(d), rows(mix_c.shape[1]), rows(mix_r.shape[1]),
                  pl.BlockSpec(wo.shape, lambda i: (0, 0), pipeline_mode=pl.Buffered(1)),
                  pl.BlockSpec(g_post.shape, lambda i: (0, 0))],
        out_specs=rows(d),
        compiler_params=_cparams(("parallel",)),
        name="outproj",
    )(h, mix_c, mix_r, wo, g_post)


def _prepare_weights(g_ffn1_pre, w_ffn1_gate, w_ffn1_up, w_ffn1_down, g_ffn1_post, g_mix_pre, w_in, conv_w,
                     tshift_mu, w0, w_decay_up, a0, w_a_up, w_g_up, k_k, k_a, r_k, ln_x_w, ln_x_b, w_out,
                     g_mix_post, g_ffn2_pre, w_ffn2_gate, w_ffn2_up, w_ffn2_down, g_ffn2_post):
    rd = w0.shape[-1]
    dl, al, gl = w_decay_up.shape[0], w_a_up.shape[0], w_g_up.shape[0]
    assert dl + al == LANES and gl <= LORA_BLOCK - LANES and rd % HEAD_DIM == 0
    lora = dl + al + gl
    gl_pad = _round_up(gl, LANES)

    def row(vec):
        return vec.reshape(1, -1).astype(F32)

    def ffn_w(wg, wu, wd):
        return wg.astype(BF16), wu.astype(BF16), wd.astype(BF16)

    head = jnp.arange(MXU_DIM, dtype=jnp.int32) // HEAD_DIM
    return dict(
        ffn1=(row(g_ffn1_pre),) + ffn_w(w_ffn1_gate, w_ffn1_up, w_ffn1_down) + (row(g_ffn1_post),),
        ffn2_gains=(row(g_ffn2_pre), row(g_ffn2_post)),
        ffn2_f32=(w_ffn2_gate, w_ffn2_up, w_ffn2_down),
        g_mix_pre=row(g_mix_pre),
        w_in=jnp.pad(w_in.astype(BF16), ((0, 0), (0, LORA_BLOCK - lora))),
        lora=lora,
        conv_w=conv_w.astype(F32),
        mu=jnp.pad(row(tshift_mu), ((0, 0), (0, LORA_BLOCK - lora))),
        w0=row(w0),
        wdec=jnp.pad(w_decay_up.astype(BF16), ((0, al), (0, 0))),
        a0=row(a0),
        wa=jnp.pad(w_a_up.astype(BF16), ((dl, 0), (0, 0))),
        wg=jnp.pad(w_g_up.astype(BF16), ((0, gl_pad - gl), (0, 0))),
        k_k=row(k_k), k_a=row(k_a), r_k=row(r_k),
        bd=(head[:, None] == head[None, :]).astype(BF16),
        ln_w=row(ln_x_w), ln_b=row(ln_x_b),
        w_out=w_out.astype(BF16),
        g_mix_post=row(g_mix_post),
    )


def _layer(x3, conv_prev, shift_prev, wkv_prev, wts, ffn2_w=None):
    bsz, t, d = x3.shape
    m = bsz * t
    lora = wts["lora"]
    x = x3.reshape(m, d)
    h = _ffn(x, *wts["ffn1"])
    p = _inproj(h, wts["g_mix_pre"], wts["w_in"])
    shift_pad = jnp.pad(shift_prev.astype(F32), ((0, 0), (0, 0), (0, LORA_BLOCK - lora)))
    nh, hd = wkv_prev.shape[1], wkv_prev.shape[2]
    s0 = wkv_prev.astype(F32).transpose(0, 2, 1, 3).reshape(bsz, hd, nh * hd)
    side = () if ffn2_w is not None else wts["ffn2_f32"]
    mix_c, mix_r, new_conv, new_shift, s_new, *cast = _mixer(
        p.reshape(bsz, t, -1), conv_prev.astype(F32), shift_pad, s0, wts, side)
    if ffn2_w is None:
        ffn2_w = tuple(cast)
    new_wkv = s_new.reshape(bsz, hd, nh, hd).transpose(0, 2, 1, 3)
    h = _outproj(h, mix_c.reshape(m, -1), mix_r.reshape(m, -1), wts["w_out"], wts["g_mix_post"])
    g_pre, g_post = wts["ffn2_gains"]
    out = _ffn(h, g_pre, *ffn2_w, g_post)
    shift_dim = shift_prev.shape[-1]
    return (out.reshape(bsz, t, d), new_conv, new_shift[:, :, :shift_dim], new_wkv), ffn2_w


def kernel(x_prompt, x_sample, cache_conv, cache_shift, state_wkv, g_ffn1_pre, w_ffn1_gate, w_ffn1_up, w_ffn1_down, g_ffn1_post, g_mix_pre, w_in, conv_w, tshift_mu, w0, w_decay_up, a0, w_a_up, w_g_up, k_k, k_a, r_k, ln_x_w, ln_x_b, w_out, g_mix_post, g_ffn2_pre, w_ffn2_gate, w_ffn2_up, w_ffn2_down, g_ffn2_post):
    depth = w_in.shape[0]
    bp = x_prompt.shape[0]
    conv_rows, conv_ch = cache_conv.shape[2], cache_conv.shape[3]
    shift_dim = cache_shift.shape[-1]
    nh, hd = state_wkv.shape[2], state_wkv.shape[3]
    per_layer = (g_ffn1_pre, w_ffn1_gate, w_ffn1_up, w_ffn1_down, g_ffn1_post, g_mix_pre, w_in, conv_w,
                 tshift_mu, w0, w_decay_up, a0, w_a_up, w_g_up, k_k, k_a, r_k, ln_x_w, ln_x_b, w_out,
                 g_mix_post, g_ffn2_pre, w_ffn2_gate, w_ffn2_up, w_ffn2_down, g_ffn2_post)
    yp, ys = x_prompt, x_sample
    outs = [[] for _ in range(6)]
    for l in range(depth):
        wts = _prepare_weights(*(w[l] for w in per_layer))
        zc = jnp.zeros((bp, conv_rows, conv_ch), F32)
        zs = jnp.zeros((bp, 1, shift_dim), F32)
        zw = jnp.zeros((bp, nh, hd, hd), F32)
        (yp, c1, s1, w1), ffn2_w = _layer(yp, zc, zs, zw, wts)
        (ys, c2, s2, w2), _ = _layer(ys, cache_conv[l], cache_shift[l], state_wkv[l], wts, ffn2_w)
        for lst, val, ref in zip(outs, (c1, s1, w1, c2, s2, w2),
                                 (cache_conv, cache_shift, state_wkv, cache_conv, cache_shift, state_wkv)):
            lst.append(val.astype(ref.dtype))
    return (yp, ys) + tuple(jnp.stack(o, 0) for o in outs)
```

```python
import functools

import jax
import jax.numpy as jnp
from jax import lax
from jax.experimental import pallas as pl
from jax.experimental.pallas import tpu as pltpu

F32 = jnp.float32
BF16 = jnp.bfloat16

RMS_EPS = 1e-6
GN_EPS = 64e-5
L2_EPS = 1e-12
HEAD_DIM = 64
WKV_CHUNK = 64
MXU_DIM = 256
WKV_GROUP = MXU_DIM // HEAD_DIM
PIPE_CHUNKS = 2
LANES = 128
LORA_BLOCK = 512
VMEM_LIMIT = 56 * 1024 * 1024
TILES = dict(ffn_tm=512, ffn_tf=512, inproj_tm=1024, inproj_tn=1664, mixer_tt=256, outproj_tm=512)
RESIDENT_ROWS = 1024


def _cparams(sem):
    return pltpu.CompilerParams(dimension_semantics=sem, vmem_limit_bytes=VMEM_LIMIT)


def _round_up(n, m):
    return (n + m - 1) // m * m


def _row_tile(m, pref):
    return pref if m % pref == 0 else m


def _rows_plan(m, pref):
    if m <= RESIDENT_ROWS or m % pref:
        return m, pl.Buffered(1)
    return pref, None


def _rms_scale(x):
    return lax.rsqrt(jnp.mean(x * x, axis=-1, keepdims=True) + RMS_EPS)


def _dot(a, b):
    return jnp.dot(a, b, preferred_element_type=F32)


def _dot_nt(a, b):
    return lax.dot_general(a, b, (((1,), (1,)), ((), ())), preferred_element_type=F32)


def _dot_tn(a, b):
    return lax.dot_general(a, b, (((0,), (0,)), ((), ())), preferred_element_type=F32)


def _split2(x):
    hi = x.astype(BF16)
    lo = (x - hi.astype(F32)).astype(BF16)
    return hi, lo


def _head_sum(x, bd):
    w = bd.shape[0]
    rows = x.shape[0]
    starts = range(0, x.shape[1], w)
    stacked = jnp.concatenate([x[:, s:s + w] for s in starts], axis=0).astype(BF16)
    sums = _dot(stacked, bd)
    return jnp.concatenate([sums[i * rows:(i + 1) * rows] for i in range(len(starts))], axis=1)


def _cast_block(src_ref, dst_ref):
    n = src_ref.shape[-1]
    if dst_ref.shape[-1] == n:
        dst_ref[...] = src_ref[...].astype(dst_ref.dtype)
    else:
        dst_ref[:, :n] = src_ref[...].astype(dst_ref.dtype)
        dst_ref[:, n:] = jnp.zeros((dst_ref.shape[0], dst_ref.shape[-1] - n), dst_ref.dtype)


def _side_plan(side, steps, step_of):
    blocks, in_specs, out_specs, shapes = [], [], [], []
    for w, width in side:
        rows = w.shape[0]
        nblk = max(n for n in range(1, steps + 1) if rows % n == 0 and (rows // n) % 16 == 0)

        def index(*ids, nblk=nblk):
            return jnp.minimum(step_of(*ids), nblk - 1), 0

        blocks.append(nblk)
        in_specs.append(pl.BlockSpec((rows // nblk, w.shape[1]), index))
        out_specs.append(pl.BlockSpec((rows // nblk, width), index))
        shapes.append(jax.ShapeDtypeStruct((rows, width), BF16))
    return tuple(blocks), in_specs, out_specs, shapes


def _run_side_jobs(step_no, side_in, side_out, side_blocks):
    for src, dst, nblk in zip(side_in, side_out, side_blocks):
        pl.when(step_no < nblk)(functools.partial(_cast_block, src, dst))


def _ffn_body(x_ref, gpre_ref, wg_ref, wu_ref, wd_ref, *rest, nb_main, has_tail, side_blocks):
    nside = len(side_blocks)
    if has_tail:
        wgt_ref, wut_ref, wdt_ref = rest[:3]
        rest = rest[3:]
    gpost_ref = rest[0]
    side_in = rest[1:1 + nside]
    o_ref = rest[1 + nside]
    side_out = rest[2 + nside:2 + 2 * nside]
    xn_ref, acc_ref = rest[2 + 2 * nside:]
    j = pl.program_id(1)
    _run_side_jobs(pl.program_id(0) * pl.num_programs(1) + j, side_in, side_out, side_blocks)

    @pl.when(j == 0)
    def _init():
        x = x_ref[...]
        xn_ref[...] = (x * _rms_scale(x) * gpre_ref[...]).astype(BF16)
        acc_ref[...] = jnp.zeros_like(acc_ref)

    def accumulate(wg, wu, wd):
        xn = xn_ref[...]
        gate = _dot(xn, wg[...])
        up = _dot(xn, wu[...])
        hmid = (gate * jax.nn.sigmoid(gate) * up).astype(BF16)
        acc_ref[...] += _dot(hmid, wd[...])

    if has_tail:
        pl.when(j < nb_main)(lambda: accumulate(wg_ref, wu_ref, wd_ref))
        pl.when(j == nb_main)(lambda: accumulate(wgt_ref, wut_ref, wdt_ref))
    else:
        accumulate(wg_ref, wu_ref, wd_ref)

    @pl.when(j == pl.num_programs(1) - 1)
    def _fin():
        a = acc_ref[...]
        o_ref[...] = x_ref[...] + 0.5 * (a * _rms_scale(a) * gpost_ref[...])


def _ffn(x, g_pre, wg, wu, wd, g_post, side=()):
    m, d = x.shape
    f = wg.shape[1]
    tm, rows_mode = _rows_plan(m, TILES["ffn_tm"])
    tf = min(TILES["ffn_tf"], f)
    nb_main = f // tf
    f_main = nb_main * tf
    assert (f - f_main) % LANES == 0
    has_tail = f_main < f
    last_main = nb_main - 1
    nj = nb_main + has_tail

    def whole(arr):
        return pl.BlockSpec(arr.shape, lambda i, j: (0, 0), pipeline_mode=pl.Buffered(1))

    tails = [wg[:, f_main:], wu[:, f_main:], wd[f_main:, :]] if has_tail else []
    side_blocks, side_in, side_out, side_shapes = _side_plan(side, (m // tm) * nj, lambda i, j: i * nj + j)
    outs = pl.pallas_call(
        functools.partial(_ffn_body, nb_main=nb_main, has_tail=has_tail, side_blocks=side_blocks),
        out_shape=[jax.ShapeDtypeStruct((m, d), F32)] + side_shapes,
        grid=(m // tm, nj),
        in_specs=[
            pl.BlockSpec((tm, d), lambda i, j: (i, 0), pipeline_mode=rows_mode),
            pl.BlockSpec((1, d), lambda i, j: (0, 0)),
            pl.BlockSpec((d, tf), lambda i, j: (0, jnp.minimum(j, last_main))),
            pl.BlockSpec((d, tf), lambda i, j: (0, jnp.minimum(j, last_main))),
            pl.BlockSpec((tf, d), lambda i, j: (jnp.minimum(j, last_main), 0)),
        ] + [whole(t) for t in tails] + [pl.BlockSpec((1, d), lambda i, j: (0, 0))] + side_in,
        out_specs=[pl.BlockSpec((tm, d), lambda i, j: (i, 0), pipeline_mode=rows_mode)] + side_out,
        scratch_shapes=[pltpu.VMEM((tm, d), BF16), pltpu.VMEM((tm, d), F32)],
        compiler_params=_cparams(("arbitrary", "arbitrary")),
        name="ffn",
    )(x, g_pre, wg, wu, wd, *tails, g_post, *(w for w, _ in side))
    return outs if side else outs[0]


def _inproj_body(x_ref, g_ref, w_ref, o_ref, xn_ref):
    @pl.when(pl.program_id(1) == 0)
    def _init():
        x = x_ref[...]
        xn_ref[...] = (x * _rms_scale(x) * g_ref[...]).astype(BF16)

    o_ref[...] = _dot(xn_ref[...], w_ref[...])


def _inproj(x, g, w):
    m, d = x.shape
    n = w.shape[1]
    tm, rows_mode = _rows_plan(m, TILES["inproj_tm"])
    tn = _row_tile(n, TILES["inproj_tn"])
    return pl.pallas_call(
        _inproj_body,
        out_shape=jax.ShapeDtypeStruct((m, n), F32),
        grid=(m // tm, n // tn),
        in_specs=[
            pl.BlockSpec((tm, d), lambda i, j: (i, 0), pipeline_mode=rows_mode),
            pl.BlockSpec((1, d), lambda i, j: (0, 0)),
            pl.BlockSpec((d, tn), lambda i, j: (0, j)),
        ],
        out_specs=pl.BlockSpec((tm, tn), lambda i, j: (i, j)),
        scratch_shapes=[pltpu.VMEM((tm, d), BF16)],
        compiler_params=_cparams(("parallel", "arbitrary")),
        name="inproj",
    )(x, g, w)


def _shift_rows(x, carry_rows):
    n = len(carry_rows)
    out = pltpu.roll(x, n, axis=0)
    row = lax.broadcasted_iota(jnp.int32, x.shape, 0)
    for i, c in enumerate(carry_rows):
        out = jnp.where(row == i, c, out)
    return out


def _bd(y, mask):
    reps = mask.shape[0] // y.shape[0]
    return jnp.where(mask, jnp.concatenate([y] * reps, axis=0), jnp.zeros((), y.dtype))


def _mm_heads(x, y, mask, nt=False):
    dot = _dot_nt if nt else _dot
    return dot(x.astype(BF16), _bd(y.astype(BF16), mask))


def _round_robin(tasks):
    while tasks:
        tasks = [t for t in tasks if next(t, tasks) is not tasks]


def _mixer_body(*refs, nside, side_blocks):
    n_in = 23
    (pb_ref, pc_ref, ph_ref, pr_ref, pk_ref, pv_ref, pl_ref, cprev_ref, sprev_ref, s0_ref,
     cw_ref, mu_ref, w0_ref, wdec_ref, a0_ref, wa_ref, wg_ref, kk_ref, ka_ref, rk_ref,
     lnw_ref, lnb_ref, bd_ref) = refs[:n_in]
    side_in = refs[n_in:n_in + nside]
    mixc_ref, mixr_ref, nconv_ref, nshift_ref, sout_ref = refs[n_in + nside:n_in + nside + 5]
    side_out = refs[n_in + nside + 5:n_in + 2 * nside + 5]
    (cu_scr, cs_scr, s_scr, r_scr, lw_scr, k_scr, v_scr, kn_scr, b_scr, y_scr, bonus_scr,
     g_scr) = refs[n_in + 2 * nside + 5:]
    ti = pl.program_id(1)
    _run_side_jobs(pl.program_id(0) * pl.num_programs(1) + ti, side_in, side_out, side_blocks)
    c = WKV_CHUNK
    n = HEAD_DIM
    rd = pr_ref.shape[-1]
    nchunk = pc_ref.shape[1] // c
    wdt = WKV_GROUP * n
    ngroups = rd // wdt

    @pl.when(ti == 0)
    def _init():
        cu_scr[...] = cprev_ref[0]
        cs_scr[...] = sprev_ref[0]
        s_scr[...] = s0_ref[0]

    bd = bd_ref[...]
    row = lax.broadcasted_iota(jnp.int32, (c, c), 0)
    col = lax.broadcasted_iota(jnp.int32, (c, c), 1)
    ltri = jnp.where(row >= col, 1.0, 0.0).astype(BF16)
    prow = lax.broadcasted_iota(jnp.int32, (c, wdt), 0)
    pcol = lax.broadcasted_iota(jnp.int32, (c, wdt), 1) % n
    strict = prow > pcol
    incl = prow >= pcol
    brow = lax.broadcasted_iota(jnp.int32, (wdt, wdt), 0) // n
    bcol = lax.broadcasted_iota(jnp.int32, (wdt, wdt), 1) // n
    same_head = brow == bcol

    def prep_chunk(sl):
        u = pc_ref[0, sl, :] * ph_ref[0, sl, :]
        c0 = cu_scr[0:1, :]
        c1 = cu_scr[1:2, :]
        u2 = _shift_rows(u, [c0, c1])
        u1 = _shift_rows(u, [c1])
        conv = u2 * cw_ref[0:1, :] + u1 * cw_ref[1:2, :] + u * cw_ref[2:3, :]
        mixc_ref[0, sl, :] = (pb_ref[0, sl, :] * conv).astype(mixc_ref.dtype)
        nr = u.shape[0]
        cu_scr[...] = u[nr - 2:nr, :]
        yield

        def shifted(p, lo, hi):
            prev = _shift_rows(p, [cs_scr[0:1, lo:hi]])
            q = p + (prev - p) * mu_ref[0:1, lo:hi]
            cs_scr[0:1, lo:hi] = p[nr - 1:nr, :]
            return q

        ql = shifted(pl_ref[0, sl, :], 3 * rd, 3 * rd + LORA_BLOCK)
        ql_da = ql[:, 0:LANES]
        dec_in = w0_ref[...] + _dot(jnp.tanh(ql_da).astype(BF16), wdec_ref[...])
        a = jax.nn.sigmoid(a0_ref[...] + _dot(ql_da.astype(BF16), wa_ref[...]))
        g_scr[sl, :] = _dot(jax.nn.sigmoid(ql[:, LANES:LANES + wg_ref.shape[0]]).astype(BF16), wg_ref[...])
        yield
        z = -dec_in
        softplus = jnp.maximum(z, 0.0) + jnp.log(1.0 + jnp.exp(-jnp.abs(z)))
        w_log = -softplus - 0.5
        lw_scr[sl, :] = -jnp.exp(w_log)
        k = shifted(pk_ref[0, sl, :], rd, 2 * rd)
        kk = k * kk_ref[...]
        ssq = _head_sum(kk * kk, bd)
        yield
        kk = kk / jnp.maximum(jnp.sqrt(ssq), L2_EPS)
        kn_scr[sl, :] = kk
        b_scr[sl, :] = kk * a
        k = k * (1.0 + (a - 1.0) * ka_ref[...])
        k_scr[sl, :] = k
        r = shifted(pr_ref[0, sl, :], 0, rd)
        r_scr[sl, :] = r
        rk_sum = _head_sum(r * k * rk_ref[...], bd)
        yield
        v = shifted(pv_ref[0, sl, :], 2 * rd, 3 * rd)
        v_scr[sl, :] = v
        bonus_scr[sl, :] = rk_sum * v

    def wkv_pre(sl, ln, out):
        lw = lw_scr[sl, ln]
        l_hi, l_lo = _split2(lw)
        cum = _dot(ltri, jnp.concatenate([l_hi, l_lo], axis=1))
        cum = cum[:, :wdt] + cum[:, wdt:]
        yield
        cum_last = cum[c - 1:c, :]
        g_end = jnp.exp(cum_last - cum)
        g_inv = jnp.exp(-cum)
        kk = k_scr[sl, ln]
        vv = v_scr[sl, ln]
        bb = b_scr[sl, ln]
        ap = jnp.concatenate([kn_scr[sl, ln] * jnp.exp(cum - lw), r_scr[sl, ln] * jnp.exp(cum)], axis=0)
        g_b = _mm_heads(ap, bb * g_inv, same_head, nt=True)
        g_k = _mm_heads(ap, kk * g_inv, same_head, nt=True)
        yield
        q = jnp.where(strict, -g_b[:c], 0.0)
        lk = jnp.concatenate([jnp.where(strict, g_k[:c], 0.0), jnp.where(incl, g_k[c:], 0.0)], axis=0)
        t1v = _mm_heads(lk, vv, same_head)
        z = q
        q = _mm_heads(q, q, same_head)
        yield
        for level in range(5):
            if level < 4:
                zq_qq = _mm_heads(jnp.concatenate([z, q], axis=0), q, same_head)
                z, q = z + q + zq_qq[:c], zq_qq[c:]
            else:
                z = z + q + _mm_heads(z, q, same_head)
            yield
        out.update(ap=ap.astype(BF16), t1v=t1v, z=z.astype(BF16), irb=jnp.where(incl, g_b[c:], 0.0).astype(BF16),
                   vv=vv, kb=jnp.concatenate([kk * g_end, bb * g_end], axis=0).astype(BF16),
                   g_last=jnp.exp(cum_last))

    def wkv_post(sl, ln, pre):
        s0 = s_scr[:, ln]
        a_s = _dot_nt(pre["ap"], _bd(s0.astype(BF16), same_head))
        yield
        t1 = a_s + pre["t1v"]
        u = t1[:c] + _dot(pre["z"], _bd(t1[:c].astype(BF16), same_head))
        yield
        ub = u.astype(BF16)
        y_scr[sl, ln] = t1[c:] - _dot(pre["irb"], _bd(ub, same_head))
        vu = jnp.concatenate([pre["vv"].astype(BF16), -ub], axis=0)
        gram = jnp.where(same_head, _dot_tn(vu, pre["kb"]), 0.0)
        upd = gram[0:n]
        for h in range(1, WKV_GROUP):
            upd = upd + gram[h * n:(h + 1) * n]
        s_scr[:, ln] = s0 * pre["g_last"] + upd

    def out_chunk(sl):
        y = y_scr[sl, :]
        inv_n = 1.0 / HEAD_DIM
        mean = _head_sum(y, bd) * inv_n
        yield
        yc = y - mean
        var = _head_sum(yc * yc, bd) * inv_n
        yield
        yn = yc * lax.rsqrt(var + GN_EPS) * lnw_ref[...] + lnb_ref[...]
        mixr_ref[0, sl, :] = ((yn + bonus_scr[sl, :]) * g_scr[sl, :]).astype(mixr_ref.dtype)

    def rows(i):
        return slice(i * c, (i + 1) * c)

    def in_turn(gens):
        for gen in gens:
            yield from gen

    lanes = [slice(g * wdt, (g + 1) * wdt) for g in range(ngroups)]
    pre = [[{} for _ in lanes] for _ in range(nchunk)]
    span = PIPE_CHUNKS if nchunk % PIPE_CHUNKS == 0 else 1
    nstage = nchunk // span

    def chunks_of(stage):
        return range(stage * span, (stage + 1) * span) if 0 <= stage < nstage else ()

    for step in range(nstage + 3):
        tasks = [in_turn([wkv_post(rows(i), ln, pre[i][g]) for i in chunks_of(step - 2)])
                 for g, ln in enumerate(lanes)]
        tasks += [wkv_pre(rows(i), ln, pre[i][g]) for i in chunks_of(step - 1) for g, ln in enumerate(lanes)]
        tasks.append(in_turn([prep_chunk(rows(i)) for i in chunks_of(step)]))
        tasks.append(in_turn([out_chunk(rows(i)) for i in chunks_of(step - 3)]))
        _round_robin(tasks)

    @pl.when(ti == pl.num_programs(1) - 1)
    def _fin():
        nconv_ref[0] = cu_scr[...]
        nshift_ref[0] = cs_scr[...]
        sout_ref[0] = s_scr[...]


def _mixer(p3, conv_prev, shift_prev, s0, wts, side=()):
    bsz, t, _ = p3.shape
    cc = conv_prev.shape[-1]
    rd = wts["w0"].shape[-1]
    sp = shift_prev.shape[-1]
    tt = _row_tile(t, TILES["mixer_tt"])
    assert tt % WKV_CHUNK == 0 and rd % (WKV_GROUP * HEAD_DIM) == 0
    lora_blk = (3 * cc + 3 * rd) // LORA_BLOCK
    nt = t // tt

    def col(width, idx):
        return pl.BlockSpec((1, tt, width), lambda b, i: (b, i, idx))

    def const(arr):
        return pl.BlockSpec(arr.shape, lambda b, i: (0,) * arr.ndim)

    def per_stream(rows, width):
        return pl.BlockSpec((1, rows, width), lambda b, i: (b, 0, 0))

    side_blocks, side_in, side_out, side_shapes = _side_plan(
        [(w, w.shape[1]) for w in side], bsz * nt, lambda b, i: b * nt + i)
    names = ("conv_w", "mu", "w0", "wdec", "a0", "wa", "wg", "k_k", "k_a", "r_k", "ln_w", "ln_b", "bd")
    consts = [wts[n] for n in names]
    tok = pltpu.VMEM((tt, rd), F32)
    return pl.pallas_call(
        functools.partial(_mixer_body, nside=len(side), side_blocks=side_blocks),
        out_shape=[jax.ShapeDtypeStruct((bsz, t, cc), BF16), jax.ShapeDtypeStruct((bsz, t, rd), BF16),
                   jax.ShapeDtypeStruct((bsz, 2, cc), F32), jax.ShapeDtypeStruct((bsz, 1, sp), F32),
                   jax.ShapeDtypeStruct((bsz, HEAD_DIM, rd), F32)] + side_shapes,
        grid=(bsz, nt),
        in_specs=[col(cc, 0), col(cc, 1), col(cc, 2), col(rd, 3), col(rd, 4), col(rd, 5),
                  col(LORA_BLOCK, lora_blk),
                  per_stream(2, cc), per_stream(1, sp), per_stream(HEAD_DIM, rd)]
        + [const(c) for c in consts] + side_in,
        out_specs=[pl.BlockSpec((1, tt, cc), lambda b, i: (b, i, 0)), pl.BlockSpec((1, tt, rd), lambda b, i: (b, i, 0)),
                   per_stream(2, cc), per_stream(1, sp), per_stream(HEAD_DIM, rd)] + side_out,
        scratch_shapes=[pltpu.VMEM((2, cc), F32), pltpu.VMEM((1, sp), F32), pltpu.VMEM((HEAD_DIM, rd), F32)]
        + [tok] * 9,
        compiler_params=_cparams(("arbitrary", "arbitrary")),
        name="mixer",
    )(p3, p3, p3, p3, p3, p3, p3, conv_prev, shift_prev, s0, *consts, *side)


def _outproj_body(h_ref, mc_ref, mr_ref, wo_ref, gpost_ref, o_ref):
    cc = mc_ref.shape[-1]
    m = _dot(mc_ref[...], wo_ref[0:cc, :]) + _dot(mr_ref[...], wo_ref[cc:, :])
    o_ref[...] = h_ref[...] + m * _rms_scale(m) * gpost_ref[...]


def _outproj(h, mix_c, mix_r, wo, g_post):
    m, d = h.shape
    tm = _row_tile(m, TILES["outproj_tm"])

    def rows(width):
        return pl.BlockSpec((tm, width), lambda i: (i, 0))

    return pl.pallas_call(
        _outproj_body,
        out_shape=jax.ShapeDtypeStruct((m, d), F32),
        grid=(m // tm,),
        in_specs=[rows(d), rows(mix_c.shape[1]), rows(mix_r.shape[1]),
                  pl.BlockSpec(wo.shape, lambda i: (0, 0), pipeline_mode=pl.Buffered(1)),
                  pl.BlockSpec(g_post.shape, lambda i: (0, 0))],
        out_specs=rows(d),
        compiler_params=_cparams(("parallel",)),
        name="outproj",
    )(h, mix_c, mix_r, wo, g_post)


def _prepare_weights(g_ffn1_pre, w_ffn1_gate, w_ffn1_up, w_ffn1_down, g_ffn1_post, g_mix_pre, w_in, conv_w,
                     tshift_mu, w0, w_decay_up, a0, w_a_up, w_g_up, k_k, k_a, r_k, ln_x_w, ln_x_b, w_out,
                     g_mix_post, g_ffn2_pre, w_ffn2_gate, w_ffn2_up, w_ffn2_down, g_ffn2_post):
    rd = w0.shape[-1]
    dl, al, gl = w_decay_up.shape[0], w_a_up.shape[0], w_g_up.shape[0]
    assert dl + al == LANES and gl <= LORA_BLOCK - LANES and rd % HEAD_DIM == 0
    lora = dl + al + gl
    gl_pad = _round_up(gl, LANES)

    def row(vec):
        return vec.reshape(1, -1).astype(F32)

    def ffn_w(wg, wu, wd):
        return wg.astype(BF16), wu.astype(BF16), wd.astype(BF16)

    head = jnp.arange(MXU_DIM, dtype=jnp.int32) // HEAD_DIM
    return dict(
        ffn1=(row(g_ffn1_pre),) + ffn_w(w_ffn1_gate, w_ffn1_up, w_ffn1_down) + (row(g_ffn1_post),),
        ffn2_gains=(row(g_ffn2_pre), row(g_ffn2_post)),
        ffn2_f32=(w_ffn2_gate, w_ffn2_up, w_ffn2_down),
        g_mix_pre=row(g_mix_pre),
        w_in_f32=w_in, w_in_width=w_in.shape[1] + LORA_BLOCK - lora,
        lora=lora,
        conv_w=conv_w.astype(F32),
        mu=jnp.pad(row(tshift_mu), ((0, 0), (0, LORA_BLOCK - lora))),
        w0=row(w0),
        wdec=jnp.pad(w_decay_up.astype(BF16), ((0, al), (0, 0))),
        a0=row(a0),
        wa=jnp.pad(w_a_up.astype(BF16), ((dl, 0), (0, 0))),
        wg=jnp.pad(w_g_up.astype(BF16), ((0, gl_pad - gl), (0, 0))),
        k_k=row(k_k), k_a=row(k_a), r_k=row(r_k),
        bd=(head[:, None] == head[None, :]).astype(BF16),
        ln_w=row(ln_x_w), ln_b=row(ln_x_b),
        w_out_f32=w_out,
        g_mix_post=row(g_mix_post),
    )


def _layer(x3, conv_prev, shift_prev, wkv_prev, wts, cast=None):
    bsz, t, d = x3.shape
    m = bsz * t
    lora = wts["lora"]
    x = x3.reshape(m, d)
    if cast is None:
        w_out_f32 = wts["w_out_f32"]
        h, w_in, w_out = _ffn(x, *wts["ffn1"], side=[(wts["w_in_f32"], wts["w_in_width"]),
                                                      (w_out_f32, w_out_f32.shape[1])])
    else:
        w_in, w_out, ffn2_w = cast
        h = _ffn(x, *wts["ffn1"])
    p = _inproj(h, wts["g_mix_pre"], w_in)
    shift_pad = jnp.pad(shift_prev.astype(F32), ((0, 0), (0, 0), (0, LORA_BLOCK - lora)))
    nh, hd = wkv_prev.shape[1], wkv_prev.shape[2]
    s0 = wkv_prev.astype(F32).transpose(0, 2, 1, 3).reshape(bsz, hd, nh * hd)
    side = wts["ffn2_f32"] if cast is None else ()
    mix_c, mix_r, new_conv, new_shift, s_new, *made = _mixer(
        p.reshape(bsz, t, -1), conv_prev.astype(F32), shift_pad, s0, wts, side)
    if cast is None:
        ffn2_w = tuple(made)
    new_wkv = s_new.reshape(bsz, hd, nh, hd).transpose(0, 2, 1, 3)
    h = _outproj(h, mix_c.reshape(m, -1), mix_r.reshape(m, -1), w_out, wts["g_mix_post"])
    g_pre, g_post = wts["ffn2_gains"]
    out = _ffn(h, g_pre, *ffn2_w, g_post)
    shift_dim = shift_prev.shape[-1]
    return (out.reshape(bsz, t, d), new_conv, new_shift[:, :, :shift_dim], new_wkv), (w_in, w_out, ffn2_w)


def kernel(x_prompt, x_sample, cache_conv, cache_shift, state_wkv, g_ffn1_pre, w_ffn1_gate, w_ffn1_up, w_ffn1_down, g_ffn1_post, g_mix_pre, w_in, conv_w, tshift_mu, w0, w_decay_up, a0, w_a_up, w_g_up, k_k, k_a, r_k, ln_x_w, ln_x_b, w_out, g_mix_post, g_ffn2_pre, w_ffn2_gate, w_ffn2_up, w_ffn2_down, g_ffn2_post):
    depth = w_in.shape[0]
    bp = x_prompt.shape[0]
    conv_rows, conv_ch = cache_conv.shape[2], cache_conv.shape[3]
    shift_dim = cache_shift.shape[-1]
    nh, hd = state_wkv.shape[2], state_wkv.shape[3]
    per_layer = (g_ffn1_pre, w_ffn1_gate, w_ffn1_up, w_ffn1_down, g_ffn1_post, g_mix_pre, w_in, conv_w,
                 tshift_mu, w0, w_decay_up, a0, w_a_up, w_g_up, k_k, k_a, r_k, ln_x_w, ln_x_b, w_out,
                 g_mix_post, g_ffn2_pre, w_ffn2_gate, w_ffn2_up, w_ffn2_down, g_ffn2_post)
    yp, ys = x_prompt, x_sample
    outs = [[] for _ in range(6)]
    for l in range(depth):
        wts = _prepare_weights(*(w[l] for w in per_layer))
        zc = jnp.zeros((bp, conv_rows, conv_ch), F32)
        zs = jnp.zeros((bp, 1, shift_dim), F32)
        zw = jnp.zeros((bp, nh, hd, hd), F32)
        (yp, c1, s1, w1), cast = _layer(yp, zc, zs, zw, wts)
        (ys, c2, s2, w2), _ = _layer(ys, cache_conv[l], cache_shift[l], state_wkv[l], wts, cast)
        for lst, val, ref in zip(outs, (c1, s1, w1, c2, s2, w2),
                                 (cache_conv, cache_shift, state_wkv, cache_conv, cache_shift, state_wkv)):
            lst.append(val.astype(ref.dtype))
    return (yp, ys) + tuple(jnp.stack(o, 0) for o in outs)
```

```python
import functools

import jax
import jax.numpy as jnp
from jax import lax
from jax.experimental import pallas as pl
from jax.experimental.pallas import tpu as pltpu

F32 = jnp.float32
BF16 = jnp.bfloat16

RMS_EPS = 1e-6
GN_EPS = 64e-5
L2_EPS = 1e-12
HEAD_DIM = 64
WKV_CHUNK = 64
MXU_DIM = 256
WKV_GROUP = MXU_DIM // HEAD_DIM
PIPE_CHUNKS = 2
LANES = 128
LORA_BLOCK = 512
VMEM_LIMIT = 56 * 1024 * 1024
TILES = dict(ffn_tm=512, ffn_tf=512, inproj_tm=1024, inproj_tn=1664, mixer_tt=256, outproj_tm=512)
RESIDENT_ROWS = 1024


def _cparams(sem):
    return pltpu.CompilerParams(dimension_semantics=sem, vmem_limit_bytes=VMEM_LIMIT)


def _round_up(n, m):
    return (n + m - 1) // m * m


def _row_tile(m, pref):
    return pref if m % pref == 0 else m


def _rows_plan(m, pref):
    if m <= RESIDENT_ROWS or m % pref:
        return m, pl.Buffered(1)
    return pref, None


def _rms_scale(x):
    return lax.rsqrt(jnp.mean(x * x, axis=-1, keepdims=True) + RMS_EPS)


def _dot(a, b):
    return jnp.dot(a, b, preferred_element_type=F32)


def _dot_nt(a, b):
    return lax.dot_general(a, b, (((1,), (1,)), ((), ())), preferred_element_type=F32)


def _dot_tn(a, b):
    return lax.dot_general(a, b, (((0,), (0,)), ((), ())), preferred_element_type=F32)


def _split2(x):
    hi = x.astype(BF16)
    lo = (x - hi.astype(F32)).astype(BF16)
    return hi, lo


def _head_sum(x, bd):
    w = bd.shape[0]
    rows = x.shape[0]
    starts = range(0, x.shape[1], w)
    stacked = jnp.concatenate([x[:, s:s + w] for s in starts], axis=0).astype(BF16)
    sums = _dot(stacked, bd)
    return jnp.concatenate([sums[i * rows:(i + 1) * rows] for i in range(len(starts))], axis=1)


def _ffn_body(x_ref, gpre_ref, wg_ref, wu_ref, wd_ref, *rest, nb_main, has_tail):
    if has_tail:
        wgt_ref, wut_ref, wdt_ref, gpost_ref, o_ref, xn_ref, acc_ref = rest
    else:
        gpost_ref, o_ref, xn_ref, acc_ref = rest
    j = pl.program_id(1)

    @pl.when(j == 0)
    def _init():
        x = x_ref[...]
        xn_ref[...] = (x * _rms_scale(x) * gpre_ref[...]).astype(BF16)
        acc_ref[...] = jnp.zeros_like(acc_ref)

    def accumulate(wg, wu, wd):
        xn = xn_ref[...]
        gate = _dot(xn, wg[...])
        up = _dot(xn, wu[...])
        hmid = (gate * jax.nn.sigmoid(gate) * up).astype(BF16)
        acc_ref[...] += _dot(hmid, wd[...])

    if has_tail:
        pl.when(j < nb_main)(lambda: accumulate(wg_ref, wu_ref, wd_ref))
        pl.when(j == nb_main)(lambda: accumulate(wgt_ref, wut_ref, wdt_ref))
    else:
        accumulate(wg_ref, wu_ref, wd_ref)

    @pl.when(j == pl.num_programs(1) - 1)
    def _fin():
        a = acc_ref[...]
        o_ref[...] = x_ref[...] + 0.5 * (a * _rms_scale(a) * gpost_ref[...])


def _ffn(x, g_pre, wg, wu, wd, g_post):
    m, d = x.shape
    f = wg.shape[1]
    tm, rows_mode = _rows_plan(m, TILES["ffn_tm"])
    tf = min(TILES["ffn_tf"], f)
    nb_main = f // tf
    f_main = nb_main * tf
    assert (f - f_main) % LANES == 0
    has_tail = f_main < f
    last_main = nb_main - 1

    def whole(arr):
        return pl.BlockSpec(arr.shape, lambda i, j: (0, 0), pipeline_mode=pl.Buffered(1))

    tails = [wg[:, f_main:], wu[:, f_main:], wd[f_main:, :]] if has_tail else []
    return pl.pallas_call(
        functools.partial(_ffn_body, nb_main=nb_main, has_tail=has_tail),
        out_shape=jax.ShapeDtypeStruct((m, d), F32),
        grid=(m // tm, nb_main + has_tail),
        in_specs=[
            pl.BlockSpec((tm, d), lambda i, j: (i, 0), pipeline_mode=rows_mode),
            pl.BlockSpec((1, d), lambda i, j: (0, 0)),
            pl.BlockSpec((d, tf), lambda i, j: (0, jnp.minimum(j, last_main))),
            pl.BlockSpec((d, tf), lambda i, j: (0, jnp.minimum(j, last_main))),
            pl.BlockSpec((tf, d), lambda i, j: (jnp.minimum(j, last_main), 0)),
        ] + [whole(t) for t in tails] + [pl.BlockSpec((1, d), lambda i, j: (0, 0))],
        out_specs=pl.BlockSpec((tm, d), lambda i, j: (i, 0), pipeline_mode=rows_mode),
        scratch_shapes=[pltpu.VMEM((tm, d), BF16), pltpu.VMEM((tm, d), F32)],
        compiler_params=_cparams(("parallel", "arbitrary")),
        name="ffn",
    )(x, g_pre, wg, wu, wd, *tails, g_post)


def _inproj_body(x_ref, g_ref, w_ref, o_ref, xn_ref):
    @pl.when(pl.program_id(1) == 0)
    def _init():
        x = x_ref[...]
        xn_ref[...] = (x * _rms_scale(x) * g_ref[...]).astype(BF16)

    o_ref[...] = _dot(xn_ref[...], w_ref[...])


def _inproj(x, g, w):
    m, d = x.shape
    n = w.shape[1]
    tm, rows_mode = _rows_plan(m, TILES["inproj_tm"])
    tn = _row_tile(n, TILES["inproj_tn"])
    return pl.pallas_call(
        _inproj_body,
        out_shape=jax.ShapeDtypeStruct((m, n), F32),
        grid=(m // tm, n // tn),
        in_specs=[
            pl.BlockSpec((tm, d), lambda i, j: (i, 0), pipeline_mode=rows_mode),
            pl.BlockSpec((1, d), lambda i, j: (0, 0)),
            pl.BlockSpec((d, tn), lambda i, j: (0, j)),
        ],
        out_specs=pl.BlockSpec((tm, tn), lambda i, j: (i, j)),
        scratch_shapes=[pltpu.VMEM((tm, d), BF16)],
        compiler_params=_cparams(("parallel", "arbitrary")),
        name="inproj",
    )(x, g, w)


def _shift_rows(x, carry_rows):
    n = len(carry_rows)
    out = pltpu.roll(x, n, axis=0)
    row = lax.broadcasted_iota(jnp.int32, x.shape, 0)
    for i, c in enumerate(carry_rows):
        out = jnp.where(row == i, c, out)
    return out


def _bd(y, mask):
    reps = mask.shape[0] // y.shape[0]
    return jnp.where(mask, jnp.concatenate([y] * reps, axis=0), jnp.zeros((), y.dtype))


def _mm_heads(x, y, mask, nt=False):
    dot = _dot_nt if nt else _dot
    return dot(x.astype(BF16), _bd(y.astype(BF16), mask))


def _cast_block(src_ref, dst_ref):
    dst_ref[...] = src_ref[...].astype(dst_ref.dtype)


def _side_blocks(rows, steps):
    return max(n for n in range(1, steps + 1) if rows % n == 0 and (rows // n) % 16 == 0)


def _round_robin(tasks):
    while tasks:
        tasks = [t for t in tasks if next(t, tasks) is not tasks]


def _mixer_body(*refs, nside, side_blocks):
    n_in = 23
    (pb_ref, pc_ref, ph_ref, pr_ref, pk_ref, pv_ref, pl_ref, cprev_ref, sprev_ref, s0_ref,
     cw_ref, mu_ref, w0_ref, wdec_ref, a0_ref, wa_ref, wg_ref, kk_ref, ka_ref, rk_ref,
     lnw_ref, lnb_ref, bd_ref) = refs[:n_in]
    side_in = refs[n_in:n_in + nside]
    mixc_ref, mixr_ref, nconv_ref, nshift_ref, sout_ref = refs[n_in + nside:n_in + nside + 5]
    side_out = refs[n_in + nside + 5:n_in + 2 * nside + 5]
    (cu_scr, cs_scr, s_scr, r_scr, lw_scr, k_scr, v_scr, kn_scr, b_scr, y_scr, bonus_scr,
     g_scr) = refs[n_in + 2 * nside + 5:]
    ti = pl.program_id(1)
    step_no = pl.program_id(0) * pl.num_programs(1) + ti
    for src, dst, nblk in zip(side_in, side_out, side_blocks):
        pl.when(step_no < nblk)(functools.partial(_cast_block, src, dst))
    c = WKV_CHUNK
    n = HEAD_DIM
    rd = pr_ref.shape[-1]
    nchunk = pc_ref.shape[1] // c
    wdt = WKV_GROUP * n
    ngroups = rd // wdt

    @pl.when(ti == 0)
    def _init():
        cu_scr[...] = cprev_ref[0]
        cs_scr[...] = sprev_ref[0]
        for h in range(rd // n):
            s_scr[:, h * n:(h + 1) * n] = s0_ref[0, h]

    bd = bd_ref[...]
    row = lax.broadcasted_iota(jnp.int32, (c, c), 0)
    col = lax.broadcasted_iota(jnp.int32, (c, c), 1)
    ltri = jnp.where(row >= col, 1.0, 0.0).astype(BF16)
    prow = lax.broadcasted_iota(jnp.int32, (c, wdt), 0)
    pcol = lax.broadcasted_iota(jnp.int32, (c, wdt), 1) % n
    strict = prow > pcol
    incl = prow >= pcol
    brow = lax.broadcasted_iota(jnp.int32, (wdt, wdt), 0) // n
    bcol = lax.broadcasted_iota(jnp.int32, (wdt, wdt), 1) // n
    same_head = brow == bcol

    def prep_chunk(sl):
        u = pc_ref[0, sl, :] * ph_ref[0, sl, :]
        c0 = cu_scr[0:1, :]
        c1 = cu_scr[1:2, :]
        u2 = _shift_rows(u, [c0, c1])
        u1 = _shift_rows(u, [c1])
        conv = u2 * cw_ref[0:1, :] + u1 * cw_ref[1:2, :] + u * cw_ref[2:3, :]
        mixc_ref[0, sl, :] = (pb_ref[0, sl, :] * conv).astype(mixc_ref.dtype)
        nr = u.shape[0]
        cu_scr[...] = u[nr - 2:nr, :]
        yield

        def shifted(p, lo, hi):
            prev = _shift_rows(p, [cs_scr[0:1, lo:hi]])
            q = p + (prev - p) * mu_ref[0:1, lo:hi]
            cs_scr[0:1, lo:hi] = p[nr - 1:nr, :]
            return q

        ql = shifted(pl_ref[0, sl, :], 3 * rd, 3 * rd + LORA_BLOCK)
        ql_da = ql[:, 0:LANES]
        dec_in = w0_ref[...] + _dot(jnp.tanh(ql_da).astype(BF16), wdec_ref[...])
        a = jax.nn.sigmoid(a0_ref[...] + _dot(ql_da.astype(BF16), wa_ref[...]))
        g_scr[sl, :] = _dot(jax.nn.sigmoid(ql[:, LANES:LANES + wg_ref.shape[0]]).astype(BF16), wg_ref[...])
        yield
        z = -dec_in
        softplus = jnp.maximum(z, 0.0) + jnp.log(1.0 + jnp.exp(-jnp.abs(z)))
        w_log = -softplus - 0.5
        lw_scr[sl, :] = -jnp.exp(w_log)
        k = shifted(pk_ref[0, sl, :], rd, 2 * rd)
        kk = k * kk_ref[...]
        ssq = _head_sum(kk * kk, bd)
        yield
        kk = kk / jnp.maximum(jnp.sqrt(ssq), L2_EPS)
        kn_scr[sl, :] = kk
        b_scr[sl, :] = kk * a
        k = k * (1.0 + (a - 1.0) * ka_ref[...])
        k_scr[sl, :] = k
        r = shifted(pr_ref[0, sl, :], 0, rd)
        r_scr[sl, :] = r
        rk_sum = _head_sum(r * k * rk_ref[...], bd)
        yield
        v = shifted(pv_ref[0, sl, :], 2 * rd, 3 * rd)
        v_scr[sl, :] = v
        bonus_scr[sl, :] = rk_sum * v

    def wkv_pre(sl, ln, out):
        lw = lw_scr[sl, ln]
        l_hi, l_lo = _split2(lw)
        cum = _dot(ltri, jnp.concatenate([l_hi, l_lo], axis=1))
        cum = cum[:, :wdt] + cum[:, wdt:]
        yield
        cum_last = cum[c - 1:c, :]
        g_end = jnp.exp(cum_last - cum)
        g_inv = jnp.exp(-cum)
        kk = k_scr[sl, ln]
        vv = v_scr[sl, ln]
        bb = b_scr[sl, ln]
        ap = jnp.concatenate([kn_scr[sl, ln] * jnp.exp(cum - lw), r_scr[sl, ln] * jnp.exp(cum)], axis=0)
        g_b = _mm_heads(ap, bb * g_inv, same_head, nt=True)
        g_k = _mm_heads(ap, kk * g_inv, same_head, nt=True)
        yield
        q = jnp.where(strict, -g_b[:c], 0.0)
        lk = jnp.concatenate([jnp.where(strict, g_k[:c], 0.0), jnp.where(incl, g_k[c:], 0.0)], axis=0)
        t1v = _mm_heads(lk, vv, same_head)
        z = q
        q = _mm_heads(q, q, same_head)
        yield
        for level in range(5):
            if level < 4:
                zq_qq = _mm_heads(jnp.concatenate([z, q], axis=0), q, same_head)
                z, q = z + q + zq_qq[:c], zq_qq[c:]
            else:
                z = z + q + _mm_heads(z, q, same_head)
            yield
        out.update(ap=ap.astype(BF16), t1v=t1v, z=z.astype(BF16), irb=jnp.where(incl, g_b[c:], 0.0).astype(BF16),
                   vv=vv, kb=jnp.concatenate([kk * g_end, bb * g_end], axis=0).astype(BF16),
                   g_last=jnp.exp(cum_last))

    def wkv_post(sl, ln, pre):
        s0 = s_scr[:, ln]
        a_s = _dot_nt(pre["ap"], _bd(s0.astype(BF16), same_head))
        yield
        t1 = a_s + pre["t1v"]
        u = t1[:c] + _dot(pre["z"], _bd(t1[:c].astype(BF16), same_head))
        yield
        ub = u.astype(BF16)
        y_scr[sl, ln] = t1[c:] - _dot(pre["irb"], _bd(ub, same_head))
        vu = jnp.concatenate([pre["vv"].astype(BF16), -ub], axis=0)
        gram = jnp.where(same_head, _dot_tn(vu, pre["kb"]), 0.0)
        upd = gram[0:n]
        for h in range(1, WKV_GROUP):
            upd = upd + gram[h * n:(h + 1) * n]
        s_scr[:, ln] = s0 * pre["g_last"] + upd

    def out_chunk(sl):
        y = y_scr[sl, :]
        inv_n = 1.0 / HEAD_DIM
        mean = _head_sum(y, bd) * inv_n
        yield
        yc = y - mean
        var = _head_sum(yc * yc, bd) * inv_n
        yield
        yn = yc * lax.rsqrt(var + GN_EPS) * lnw_ref[...] + lnb_ref[...]
        mixr_ref[0, sl, :] = ((yn + bonus_scr[sl, :]) * g_scr[sl, :]).astype(mixr_ref.dtype)

    def rows(i):
        return slice(i * c, (i + 1) * c)

    def in_turn(gens):
        for gen in gens:
            yield from gen

    lanes = [slice(g * wdt, (g + 1) * wdt) for g in range(ngroups)]
    pre = [[{} for _ in lanes] for _ in range(nchunk)]
    span = PIPE_CHUNKS if nchunk % PIPE_CHUNKS == 0 else 1
    nstage = nchunk // span

    def chunks_of(stage):
        return range(stage * span, (stage + 1) * span) if 0 <= stage < nstage else ()

    for step in range(nstage + 3):
        tasks = [in_turn([wkv_post(rows(i), ln, pre[i][g]) for i in chunks_of(step - 2)])
                 for g, ln in enumerate(lanes)]
        tasks += [wkv_pre(rows(i), ln, pre[i][g]) for i in chunks_of(step - 1) for g, ln in enumerate(lanes)]
        tasks.append(in_turn([prep_chunk(rows(i)) for i in chunks_of(step)]))
        tasks.append(in_turn([out_chunk(rows(i)) for i in chunks_of(step - 3)]))
        _round_robin(tasks)

    @pl.when(ti == pl.num_programs(1) - 1)
    def _fin():
        nconv_ref[0] = cu_scr[...]
        nshift_ref[0] = cs_scr[...]
        for h in range(rd // n):
            sout_ref[0, h] = s_scr[:, h * n:(h + 1) * n]


def _mixer(p3, conv_prev, shift_prev, s0, wts, side=()):
    bsz, t, _ = p3.shape
    cc = conv_prev.shape[-1]
    rd = wts["w0"].shape[-1]
    sp = shift_prev.shape[-1]
    tt = _row_tile(t, TILES["mixer_tt"])
    assert tt % WKV_CHUNK == 0 and rd % (WKV_GROUP * HEAD_DIM) == 0
    lora_blk = (3 * cc + 3 * rd) // LORA_BLOCK
    nt = t // tt

    def col(width, idx):
        return pl.BlockSpec((1, tt, width), lambda b, i: (b, i, idx))

    def const(arr):
        return pl.BlockSpec(arr.shape, lambda b, i: (0,) * arr.ndim)

    def per_stream(rows, width):
        return pl.BlockSpec((1, rows, width), lambda b, i: (b, 0, 0))

    state_spec = pl.BlockSpec((1,) + s0.shape[1:], lambda b, i: (b, 0, 0, 0))
    side_blocks = tuple(_side_blocks(w.shape[0], bsz * nt) for w in side)

    def side_spec(w, nblk):
        return pl.BlockSpec((w.shape[0] // nblk, w.shape[1]), lambda b, i: (jnp.minimum(b * nt + i, nblk - 1), 0))

    side_specs = [side_spec(w, nblk) for w, nblk in zip(side, side_blocks)]
    names = ("conv_w", "mu", "w0", "wdec", "a0", "wa", "wg", "k_k", "k_a", "r_k", "ln_w", "ln_b", "bd")
    consts = [wts[n] for n in names]
    tok = pltpu.VMEM((tt, rd), F32)
    return pl.pallas_call(
        functools.partial(_mixer_body, nside=len(side), side_blocks=side_blocks),
        out_shape=[jax.ShapeDtypeStruct((bsz, t, cc), BF16), jax.ShapeDtypeStruct((bsz, t, rd), BF16),
                   jax.ShapeDtypeStruct((bsz, 2, cc), F32), jax.ShapeDtypeStruct((bsz, 1, sp), F32),
                   jax.ShapeDtypeStruct(s0.shape, F32)]
        + [jax.ShapeDtypeStruct(w.shape, BF16) for w in side],
        grid=(bsz, nt),
        in_specs=[col(cc, 0), col(cc, 1), col(cc, 2), col(rd, 3), col(rd, 4), col(rd, 5),
                  col(LORA_BLOCK, lora_blk),
                  per_stream(2, cc), per_stream(1, sp), state_spec]
        + [const(c) for c in consts] + side_specs,
        out_specs=[pl.BlockSpec((1, tt, cc), lambda b, i: (b, i, 0)), pl.BlockSpec((1, tt, rd), lambda b, i: (b, i, 0)),
                   per_stream(2, cc), per_stream(1, sp), state_spec] + side_specs,
        scratch_shapes=[pltpu.VMEM((2, cc), F32), pltpu.VMEM((1, sp), F32), pltpu.VMEM((HEAD_DIM, rd), F32)]
        + [tok] * 9,
        compiler_params=_cparams(("arbitrary", "arbitrary")),
        name="mixer",
    )(p3, p3, p3, p3, p3, p3, p3, conv_prev, shift_prev, s0, *consts, *side)


def _outproj_body(h_ref, mc_ref, mr_ref, wo_ref, gpost_ref, o_ref):
    cc = mc_ref.shape[-1]
    m = _dot(mc_ref[...], wo_ref[0:cc, :]) + _dot(mr_ref[...], wo_ref[cc:, :])
    o_ref[...] = h_ref[...] + m * _rms_scale(m) * gpost_ref[...]


def _outproj(h, mix_c, mix_r, wo, g_post):
    m, d = h.shape
    tm = _row_tile(m, TILES["outproj_tm"])

    def rows(width):
        return pl.BlockSpec((tm, width), lambda i: (i, 0))

    return pl.pallas_call(
        _outproj_body,
        out_shape=jax.ShapeDtypeStruct((m, d), F32),
        grid=(m // tm,),
        in_specs=[rows(d), rows(mix_c.shape[1]), rows(mix_r.shape[1]),
                  pl.BlockSpec(wo.shape, lambda i: (0, 0), pipeline_mode=pl.Buffered(1)),
                  pl.BlockSpec(g_post.shape, lambda i: (0, 0))],
        out_specs=rows(d),
        compiler_params=_cparams(("parallel",)),
        name="outproj",
    )(h, mix_c, mix_r, wo, g_post)


def _prepare_weights(g_ffn1_pre, w_ffn1_gate, w_ffn1_up, w_ffn1_down, g_ffn1_post, g_mix_pre, w_in, conv_w,
                     tshift_mu, w0, w_decay_up, a0, w_a_up, w_g_up, k_k, k_a, r_k, ln_x_w, ln_x_b, w_out,
                     g_mix_post, g_ffn2_pre, w_ffn2_gate, w_ffn2_up, w_ffn2_down, g_ffn2_post):
    rd = w0.shape[-1]
    dl, al, gl = w_decay_up.shape[0], w_a_up.shape[0], w_g_up.shape[0]
    assert dl + al == LANES and gl <= LORA_BLOCK - LANES and rd % HEAD_DIM == 0
    lora = dl + al + gl
    gl_pad = _round_up(gl, LANES)

    def row(vec):
        return vec.reshape(1, -1).astype(F32)

    def ffn_w(wg, wu, wd):
        return wg.astype(BF16), wu.astype(BF16), wd.astype(BF16)

    head = jnp.arange(MXU_DIM, dtype=jnp.int32) // HEAD_DIM
    return dict(
        ffn1=(row(g_ffn1_pre),) + ffn_w(w_ffn1_gate, w_ffn1_up, w_ffn1_down) + (row(g_ffn1_post),),
        ffn2_gains=(row(g_ffn2_pre), row(g_ffn2_post)),
        ffn2_f32=(w_ffn2_gate, w_ffn2_up, w_ffn2_down),
        g_mix_pre=row(g_mix_pre),
        w_in=jnp.pad(w_in.astype(BF16), ((0, 0), (0, LORA_BLOCK - lora))),
        lora=lora,
        conv_w=conv_w.astype(F32),
        mu=jnp.pad(row(tshift_mu), ((0, 0), (0, LORA_BLOCK - lora))),
        w0=row(w0),
        wdec=jnp.pad(w_decay_up.astype(BF16), ((0, al), (0, 0))),
        a0=row(a0),
        wa=jnp.pad(w_a_up.astype(BF16), ((dl, 0), (0, 0))),
        wg=jnp.pad(w_g_up.astype(BF16), ((0, gl_pad - gl), (0, 0))),
        k_k=row(k_k), k_a=row(k_a), r_k=row(r_k),
        bd=(head[:, None] == head[None, :]).astype(BF16),
        ln_w=row(ln_x_w), ln_b=row(ln_x_b),
        w_out=w_out.astype(BF16),
        g_mix_post=row(g_mix_post),
    )


def _layer(x3, conv_prev, shift_prev, wkv_prev, wts, ffn2_w=None):
    bsz, t, d = x3.shape
    m = bsz * t
    lora = wts["lora"]
    x = x3.reshape(m, d)
    h = _ffn(x, *wts["ffn1"])
    p = _inproj(h, wts["g_mix_pre"], wts["w_in"])
    shift_pad = jnp.pad(shift_prev.astype(F32), ((0, 0), (0, 0), (0, LORA_BLOCK - lora)))
    side = () if ffn2_w is not None else wts["ffn2_f32"]
    mix_c, mix_r, new_conv, new_shift, new_wkv, *cast = _mixer(
        p.reshape(bsz, t, -1), conv_prev.astype(F32), shift_pad, wkv_prev.astype(F32), wts, side)
    if ffn2_w is None:
        ffn2_w = tuple(cast)
    h = _outproj(h, mix_c.reshape(m, -1), mix_r.reshape(m, -1), wts["w_out"], wts["g_mix_post"])
    g_pre, g_post = wts["ffn2_gains"]
    out = _ffn(h, g_pre, *ffn2_w, g_post)
    shift_dim = shift_prev.shape[-1]
    return (out.reshape(bsz, t, d), new_conv, new_shift[:, :, :shift_dim], new_wkv), ffn2_w


def kernel(x_prompt, x_sample, cache_conv, cache_shift, state_wkv, g_ffn1_pre, w_ffn1_gate, w_ffn1_up, w_ffn1_down, g_ffn1_post, g_mix_pre, w_in, conv_w, tshift_mu, w0, w_decay_up, a0, w_a_up, w_g_up, k_k, k_a, r_k, ln_x_w, ln_x_b, w_out, g_mix_post, g_ffn2_pre, w_ffn2_gate, w_ffn2_up, w_ffn2_down, g_ffn2_post):
    depth = w_in.shape[0]
    bp = x_prompt.shape[0]
    conv_rows, conv_ch = cache_conv.shape[2], cache_conv.shape[3]
    shift_dim = cache_shift.shape[-1]
    nh, hd = state_wkv.shape[2], state_wkv.shape[3]
    per_layer = (g_ffn1_pre, w_ffn1_gate, w_ffn1_up, w_ffn1_down, g_ffn1_post, g_mix_pre, w_in, conv_w,
                 tshift_mu, w0, w_decay_up, a0, w_a_up, w_g_up, k_k, k_a, r_k, ln_x_w, ln_x_b, w_out,
                 g_mix_post, g_ffn2_pre, w_ffn2_gate, w_ffn2_up, w_ffn2_down, g_ffn2_post)
    yp, ys = x_prompt, x_sample
    outs = [[] for _ in range(6)]
    for l in range(depth):
        wts = _prepare_weights(*(w[l] for w in per_layer))
        zc = jnp.zeros((bp, conv_rows, conv_ch), F32)
        zs = jnp.zeros((bp, 1, shift_dim), F32)
        zw = jnp.zeros((bp, nh, hd, hd), F32)
        (yp, c1, s1, w1), ffn2_w = _layer(yp, zc, zs, zw, wts)
        (ys, c2, s2, w2), _ = _layer(ys, cache_conv[l], cache_shift[l], state_wkv[l], wts, ffn2_w)
        for lst, val, ref in zip(outs, (c1, s1, w1, c2, s2, w2),
                                 (cache_conv, cache_shift, state_wkv, cache_conv, cache_shift, state_wkv)):
            lst.append(val.astype(ref.dtype))
    return (yp, ys) + tuple(jnp.stack(o, 0) for o in outs)
```

```python
import functools

import jax
import jax.numpy as jnp
from jax import lax
from jax.experimental import pallas as pl
from jax.experimental.pallas import tpu as pltpu

F32 = jnp.float32
BF16 = jnp.bfloat16

RMS_EPS = 1e-6
GN_EPS = 64e-5
L2_EPS = 1e-12
HEAD_DIM = 64
WKV_CHUNK = 64
MXU_DIM = 256
WKV_GROUP = MXU_DIM // HEAD_DIM
PIPE_CHUNKS = 2
LANES = 128
LORA_BLOCK = 512
VMEM_LIMIT = 56 * 1024 * 1024
TILES = dict(ffn_tm=512, ffn_tf=512, inproj_tm=1024, inproj_tn=1664, mixer_tt=256, outproj_tm=512)
RESIDENT_ROWS = 1024


def _cparams(sem):
    return pltpu.CompilerParams(dimension_semantics=sem, vmem_limit_bytes=VMEM_LIMIT)


def _round_up(n, m):
    return (n + m - 1) // m * m


def _row_tile(m, pref):
    return pref if m % pref == 0 else m


def _rows_plan(m, pref):
    if m <= RESIDENT_ROWS or m % pref:
        return m, pl.Buffered(1)
    return pref, None


def _rms_scale(x):
    return lax.rsqrt(jnp.mean(x * x, axis=-1, keepdims=True) + RMS_EPS)


def _dot(a, b):
    return jnp.dot(a, b, preferred_element_type=F32)


def _dot_nt(a, b):
    return lax.dot_general(a, b, (((1,), (1,)), ((), ())), preferred_element_type=F32)


def _dot_tn(a, b):
    return lax.dot_general(a, b, (((0,), (0,)), ((), ())), preferred_element_type=F32)


def _split2(x):
    hi = x.astype(BF16)
    lo = (x - hi.astype(F32)).astype(BF16)
    return hi, lo


def _head_sum(x, bd):
    w = bd.shape[0]
    rows = x.shape[0]
    starts = range(0, x.shape[1], w)
    stacked = jnp.concatenate([x[:, s:s + w] for s in starts], axis=0).astype(BF16)
    sums = _dot(stacked, bd)
    return jnp.concatenate([sums[i * rows:(i + 1) * rows] for i in range(len(starts))], axis=1)


def _ffn_body(x_ref, gpre_ref, wg_ref, wu_ref, wd_ref, *rest, nb_main, tail_pieces):
    npiece = 3 * tail_pieces
    wgt, wut, wdt = (rest[k * tail_pieces:(k + 1) * tail_pieces] for k in range(3))
    gpost_ref, o_ref, xn_ref, acc_ref = rest[npiece:]
    j = pl.program_id(1)

    @pl.when(j == 0)
    def _init():
        x = x_ref[...]
        xn_ref[...] = (x * _rms_scale(x) * gpre_ref[...]).astype(BF16)
        acc_ref[...] = jnp.zeros_like(acc_ref)

    def accumulate(wg, wu, wd):
        xn = xn_ref[...]
        gate = _dot(xn, wg)
        up = _dot(xn, wu)
        hmid = (gate * jax.nn.sigmoid(gate) * up).astype(BF16)
        acc_ref[...] += _dot(hmid, wd)

    def joined(pieces, axis):
        return jnp.concatenate([p[...] for p in pieces], axis=axis)

    if tail_pieces:
        pl.when(j < nb_main)(lambda: accumulate(wg_ref[...], wu_ref[...], wd_ref[...]))
        pl.when(j == nb_main)(lambda: accumulate(joined(wgt, 1), joined(wut, 1), joined(wdt, 0)))
    else:
        accumulate(wg_ref[...], wu_ref[...], wd_ref[...])

    @pl.when(j == pl.num_programs(1) - 1)
    def _fin():
        a = acc_ref[...]
        o_ref[...] = x_ref[...] + 0.5 * (a * _rms_scale(a) * gpost_ref[...])


def _ffn(x, g_pre, wg, wu, wd, g_post):
    m, d = x.shape
    f = wg.shape[1]
    tm, rows_mode = _rows_plan(m, TILES["ffn_tm"])
    tf = min(TILES["ffn_tf"], f)
    nb_main = f // tf
    f_main = nb_main * tf
    assert (f - f_main) % LANES == 0
    tail_pieces = (f - f_main) // LANES
    has_tail = tail_pieces > 0
    last_main = nb_main - 1
    first_piece = f_main // LANES

    def col_piece(k):
        return pl.BlockSpec((d, LANES), lambda i, j: (0, first_piece + k), pipeline_mode=pl.Buffered(1))

    def row_piece(k):
        return pl.BlockSpec((LANES, d), lambda i, j: (first_piece + k, 0), pipeline_mode=pl.Buffered(1))

    pieces = range(tail_pieces)
    tail_specs = [col_piece(k) for k in pieces] * 2 + [row_piece(k) for k in pieces]
    tails = [wg] * tail_pieces + [wu] * tail_pieces + [wd] * tail_pieces
    return pl.pallas_call(
        functools.partial(_ffn_body, nb_main=nb_main, tail_pieces=tail_pieces),
        out_shape=jax.ShapeDtypeStruct((m, d), F32),
        grid=(m // tm, nb_main + has_tail),
        in_specs=[
            pl.BlockSpec((tm, d), lambda i, j: (i, 0), pipeline_mode=rows_mode),
            pl.BlockSpec((1, d), lambda i, j: (0, 0)),
            pl.BlockSpec((d, tf), lambda i, j: (0, jnp.minimum(j, last_main))),
            pl.BlockSpec((d, tf), lambda i, j: (0, jnp.minimum(j, last_main))),
            pl.BlockSpec((tf, d), lambda i, j: (jnp.minimum(j, last_main), 0)),
        ] + tail_specs + [pl.BlockSpec((1, d), lambda i, j: (0, 0))],
        out_specs=pl.BlockSpec((tm, d), lambda i, j: (i, 0), pipeline_mode=rows_mode),
        scratch_shapes=[pltpu.VMEM((tm, d), BF16), pltpu.VMEM((tm, d), F32)],
        compiler_params=_cparams(("parallel", "arbitrary")),
        name="ffn",
    )(x, g_pre, wg, wu, wd, *tails, g_post)


def _inproj_body(x_ref, g_ref, w_ref, o_ref, xn_ref):
    @pl.when(pl.program_id(1) == 0)
    def _init():
        x = x_ref[...]
        xn_ref[...] = (x * _rms_scale(x) * g_ref[...]).astype(BF16)

    o_ref[...] = _dot(xn_ref[...], w_ref[...])


def _inproj(x, g, w):
    m, d = x.shape
    n = w.shape[1]
    tm, rows_mode = _rows_plan(m, TILES["inproj_tm"])
    tn = _row_tile(n, TILES["inproj_tn"])
    return pl.pallas_call(
        _inproj_body,
        out_shape=jax.ShapeDtypeStruct((m, n), F32),
        grid=(m // tm, n // tn),
        in_specs=[
            pl.BlockSpec((tm, d), lambda i, j: (i, 0), pipeline_mode=rows_mode),
            pl.BlockSpec((1, d), lambda i, j: (0, 0)),
            pl.BlockSpec((d, tn), lambda i, j: (0, j)),
        ],
        out_specs=pl.BlockSpec((tm, tn), lambda i, j: (i, j)),
        scratch_shapes=[pltpu.VMEM((tm, d), BF16)],
        compiler_params=_cparams(("parallel", "arbitrary")),
        name="inproj",
    )(x, g, w)


def _shift_rows(x, carry_rows):
    n = len(carry_rows)
    out = pltpu.roll(x, n, axis=0)
    row = lax.broadcasted_iota(jnp.int32, x.shape, 0)
    for i, c in enumerate(carry_rows):
        out = jnp.where(row == i, c, out)
    return out


def _bd(y, mask):
    reps = mask.shape[0] // y.shape[0]
    return jnp.where(mask, jnp.concatenate([y] * reps, axis=0), jnp.zeros((), y.dtype))


def _mm_heads(x, y, mask, nt=False):
    dot = _dot_nt if nt else _dot
    return dot(x.astype(BF16), _bd(y.astype(BF16), mask))


def _cast_block(src_ref, dst_ref):
    dst_ref[...] = src_ref[...].astype(dst_ref.dtype)


def _side_blocks(rows, steps):
    return max(n for n in range(1, steps + 1) if rows % n == 0 and (rows // n) % 16 == 0)


def _round_robin(tasks):
    while tasks:
        tasks = [t for t in tasks if next(t, tasks) is not tasks]


def _mixer_body(*refs, nside, side_blocks):
    n_in = 23
    (pb_ref, pc_ref, ph_ref, pr_ref, pk_ref, pv_ref, pl_ref, cprev_ref, sprev_ref, s0_ref,
     cw_ref, mu_ref, w0_ref, wdec_ref, a0_ref, wa_ref, wg_ref, kk_ref, ka_ref, rk_ref,
     lnw_ref, lnb_ref, bd_ref) = refs[:n_in]
    side_in = refs[n_in:n_in + nside]
    mixc_ref, mixr_ref, nconv_ref, nshift_ref, sout_ref = refs[n_in + nside:n_in + nside + 5]
    side_out = refs[n_in + nside + 5:n_in + 2 * nside + 5]
    (cu_scr, cs_scr, s_scr, r_scr, lw_scr, k_scr, v_scr, kn_scr, b_scr, y_scr, bonus_scr,
     g_scr) = refs[n_in + 2 * nside + 5:]
    ti = pl.program_id(1)
    step_no = pl.program_id(0) * pl.num_programs(1) + ti
    for src, dst, nblk in zip(side_in, side_out, side_blocks):
        pl.when(step_no < nblk)(functools.partial(_cast_block, src, dst))
    c = WKV_CHUNK
    n = HEAD_DIM
    rd = pr_ref.shape[-1]
    nchunk = pc_ref.shape[1] // c
    wdt = WKV_GROUP * n
    ngroups = rd // wdt

    @pl.when(ti == 0)
    def _init():
        cu_scr[...] = cprev_ref[0]
        cs_scr[...] = sprev_ref[0]
        for h in range(rd // n):
            s_scr[:, h * n:(h + 1) * n] = s0_ref[0, h]

    bd = bd_ref[...]
    row = lax.broadcasted_iota(jnp.int32, (c, c), 0)
    col = lax.broadcasted_iota(jnp.int32, (c, c), 1)
    ltri = jnp.where(row >= col, 1.0, 0.0).astype(BF16)
    prow = lax.broadcasted_iota(jnp.int32, (c, wdt), 0)
    pcol = lax.broadcasted_iota(jnp.int32, (c, wdt), 1) % n
    strict = prow > pcol
    incl = prow >= pcol
    brow = lax.broadcasted_iota(jnp.int32, (wdt, wdt), 0) // n
    bcol = lax.broadcasted_iota(jnp.int32, (wdt, wdt), 1) // n
    same_head = brow == bcol

    def prep_chunk(sl):
        u = pc_ref[0, sl, :] * ph_ref[0, sl, :]
        c0 = cu_scr[0:1, :]
        c1 = cu_scr[1:2, :]
        u2 = _shift_rows(u, [c0, c1])
        u1 = _shift_rows(u, [c1])
        conv = u2 * cw_ref[0:1, :] + u1 * cw_ref[1:2, :] + u * cw_ref[2:3, :]
        mixc_ref[0, sl, :] = (pb_ref[0, sl, :] * conv).astype(mixc_ref.dtype)
        nr = u.shape[0]
        cu_scr[...] = u[nr - 2:nr, :]
        yield

        def shifted(p, lo, hi):
            prev = _shift_rows(p, [cs_scr[0:1, lo:hi]])
            q = p + (prev - p) * mu_ref[0:1, lo:hi]
            cs_scr[0:1, lo:hi] = p[nr - 1:nr, :]
            return q

        ql = shifted(pl_ref[0, sl, :], 3 * rd, 3 * rd + LORA_BLOCK)
        ql_da = ql[:, 0:LANES]
        dec_in = w0_ref[...] + _dot(jnp.tanh(ql_da).astype(BF16), wdec_ref[...])
        a = jax.nn.sigmoid(a0_ref[...] + _dot(ql_da.astype(BF16), wa_ref[...]))
        g_scr[sl, :] = _dot(jax.nn.sigmoid(ql[:, LANES:LANES + wg_ref.shape[0]]).astype(BF16), wg_ref[...])
        yield
        z = -dec_in
        softplus = jnp.maximum(z, 0.0) + jnp.log(1.0 + jnp.exp(-jnp.abs(z)))
        w_log = -softplus - 0.5
        lw_scr[sl, :] = -jnp.exp(w_log)
        k = shifted(pk_ref[0, sl, :], rd, 2 * rd)
        kk = k * kk_ref[...]
        ssq = _head_sum(kk * kk, bd)
        yield
        kk = kk / jnp.maximum(jnp.sqrt(ssq), L2_EPS)
        kn_scr[sl, :] = kk
        b_scr[sl, :] = kk * a
        k = k * (1.0 + (a - 1.0) * ka_ref[...])
        k_scr[sl, :] = k
        r = shifted(pr_ref[0, sl, :], 0, rd)
        r_scr[sl, :] = r
        rk_sum = _head_sum(r * k * rk_ref[...], bd)
        yield
        v = shifted(pv_ref[0, sl, :], 2 * rd, 3 * rd)
        v_scr[sl, :] = v
        bonus_scr[sl, :] = rk_sum * v

    def wkv_pre(sl, ln, out):
        lw = lw_scr[sl, ln]
        l_hi, l_lo = _split2(lw)
        cum = _dot(ltri, jnp.concatenate([l_hi, l_lo], axis=1))
        cum = cum[:, :wdt] + cum[:, wdt:]
        yield
        cum_last = cum[c - 1:c, :]
        g_end = jnp.exp(cum_last - cum)
        g_inv = jnp.exp(-cum)
        kk = k_scr[sl, ln]
        vv = v_scr[sl, ln]
        bb = b_scr[sl, ln]
        ap = jnp.concatenate([kn_scr[sl, ln] * jnp.exp(cum - lw), r_scr[sl, ln] * jnp.exp(cum)], axis=0)
        g_b = _mm_heads(ap, bb * g_inv, same_head, nt=True)
        g_k = _mm_heads(ap, kk * g_inv, same_head, nt=True)
        yield
        q = jnp.where(strict, -g_b[:c], 0.0)
        lk = jnp.concatenate([jnp.where(strict, g_k[:c], 0.0), jnp.where(incl, g_k[c:], 0.0)], axis=0)
        t1v = _mm_heads(lk, vv, same_head)
        z = q
        q = _mm_heads(q, q, same_head)
        yield
        for level in range(5):
            if level < 4:
                zq_qq = _mm_heads(jnp.concatenate([z, q], axis=0), q, same_head)
                z, q = z + q + zq_qq[:c], zq_qq[c:]
            else:
                z = z + q + _mm_heads(z, q, same_head)
            yield
        out.update(ap=ap.astype(BF16), t1v=t1v, z=z.astype(BF16), irb=jnp.where(incl, g_b[c:], 0.0).astype(BF16),
                   vv=vv, kb=jnp.concatenate([kk * g_end, bb * g_end], axis=0).astype(BF16),
                   g_last=jnp.exp(cum_last))

    def wkv_post(sl, ln, pre):
        s0 = s_scr[:, ln]
        a_s = _dot_nt(pre["ap"], _bd(s0.astype(BF16), same_head))
        yield
        t1 = a_s + pre["t1v"]
        u = t1[:c] + _dot(pre["z"], _bd(t1[:c].astype(BF16), same_head))
        yield
        ub = u.astype(BF16)
        y_scr[sl, ln] = t1[c:] - _dot(pre["irb"], _bd(ub, same_head))
        vu = jnp.concatenate([pre["vv"].astype(BF16), -ub], axis=0)
        gram = jnp.where(same_head, _dot_tn(vu, pre["kb"]), 0.0)
        upd = gram[0:n]
        for h in range(1, WKV_GROUP):
            upd = upd + gram[h * n:(h + 1) * n]
        s_scr[:, ln] = s0 * pre["g_last"] + upd

    def out_chunk(sl):
        y = y_scr[sl, :]
        inv_n = 1.0 / HEAD_DIM
        mean = _head_sum(y, bd) * inv_n
        yield
        yc = y - mean
        var = _head_sum(yc * yc, bd) * inv_n
        yield
        yn = yc * lax.rsqrt(var + GN_EPS) * lnw_ref[...] + lnb_ref[...]
        mixr_ref[0, sl, :] = ((yn + bonus_scr[sl, :]) * g_scr[sl, :]).astype(mixr_ref.dtype)

    def rows(i):
        return slice(i * c, (i + 1) * c)

    def in_turn(gens):
        for gen in gens:
            yield from gen

    lanes = [slice(g * wdt, (g + 1) * wdt) for g in range(ngroups)]
    pre = [[{} for _ in lanes] for _ in range(nchunk)]
    span = PIPE_CHUNKS if nchunk % PIPE_CHUNKS == 0 else 1
    nstage = nchunk // span

    def chunks_of(stage):
        return range(stage * span, (stage + 1) * span) if 0 <= stage < nstage else ()

    for step in range(nstage + 3):
        tasks = [in_turn([wkv_post(rows(i), ln, pre[i][g]) for i in chunks_of(step - 2)])
                 for g, ln in enumerate(lanes)]
        tasks += [wkv_pre(rows(i), ln, pre[i][g]) for i in chunks_of(step - 1) for g, ln in enumerate(lanes)]
        tasks.append(in_turn([prep_chunk(rows(i)) for i in chunks_of(step)]))
        tasks.append(in_turn([out_chunk(rows(i)) for i in chunks_of(step - 3)]))
        _round_robin(tasks)

    @pl.when(ti == pl.num_programs(1) - 1)
    def _fin():
        nconv_ref[0] = cu_scr[...]
        nshift_ref[0] = cs_scr[...]
        for h in range(rd // n):
            sout_ref[0, h] = s_scr[:, h * n:(h + 1) * n]


def _mixer(p3, conv_prev, shift_prev, s0, wts, side=()):
    bsz, t, _ = p3.shape
    cc = conv_prev.shape[-1]
    rd = wts["w0"].shape[-1]
    sp = shift_prev.shape[-1]
    tt = _row_tile(t, TILES["mixer_tt"])
    assert tt % WKV_CHUNK == 0 and rd % (WKV_GROUP * HEAD_DIM) == 0
    lora_blk = (3 * cc + 3 * rd) // LORA_BLOCK
    nt = t // tt

    def col(width, idx):
        return pl.BlockSpec((1, tt, width), lambda b, i: (b, i, idx))

    def const(arr):
        return pl.BlockSpec(arr.shape, lambda b, i: (0,) * arr.ndim)

    def per_stream(rows, width):
        return pl.BlockSpec((1, rows, width), lambda b, i: (b, 0, 0))

    state_spec = pl.BlockSpec((1,) + s0.shape[1:], lambda b, i: (b, 0, 0, 0))
    side_blocks = tuple(_side_blocks(w.shape[0], bsz * nt) for w in side)

    def side_spec(w, nblk):
        return pl.BlockSpec((w.shape[0] // nblk, w.shape[1]), lambda b, i: (jnp.minimum(b * nt + i, nblk - 1), 0))

    side_specs = [side_spec(w, nblk) for w, nblk in zip(side, side_blocks)]
    names = ("conv_w", "mu", "w0", "wdec", "a0", "wa", "wg", "k_k", "k_a", "r_k", "ln_w", "ln_b", "bd")
    consts = [wts[n] for n in names]
    tok = pltpu.VMEM((tt, rd), F32)
    return pl.pallas_call(
        functools.partial(_mixer_body, nside=len(side), side_blocks=side_blocks),
        out_shape=[jax.ShapeDtypeStruct((bsz, t, cc), BF16), jax.ShapeDtypeStruct((bsz, t, rd), BF16),
                   jax.ShapeDtypeStruct((bsz, 2, cc), F32), jax.ShapeDtypeStruct((bsz, 1, sp), F32),
                   jax.ShapeDtypeStruct(s0.shape, F32)]
        + [jax.ShapeDtypeStruct(w.shape, BF16) for w in side],
        grid=(bsz, nt),
        in_specs=[col(cc, 0), col(cc, 1), col(cc, 2), col(rd, 3), col(rd, 4), col(rd, 5),
                  col(LORA_BLOCK, lora_blk),
                  per_stream(2, cc), per_stream(1, sp), state_spec]
        + [const(c) for c in consts] + side_specs,
        out_specs=[pl.BlockSpec((1, tt, cc), lambda b, i: (b, i, 0)), pl.BlockSpec((1, tt, rd), lambda b, i: (b, i, 0)),
                   per_stream(2, cc), per_stream(1, sp), state_spec] + side_specs,
        scratch_shapes=[pltpu.VMEM((2, cc), F32), pltpu.VMEM((1, sp), F32), pltpu.VMEM((HEAD_DIM, rd), F32)]
        + [tok] * 9,
        compiler_params=_cparams(("arbitrary", "arbitrary")),
        name="mixer",
    )(p3, p3, p3, p3, p3, p3, p3, conv_prev, shift_prev, s0, *consts, *side)


def _outproj_body(h_ref, mc_ref, mr_ref, wo_ref, gpost_ref, o_ref):
    cc = mc_ref.shape[-1]
    m = _dot(mc_ref[...], wo_ref[0:cc, :]) + _dot(mr_ref[...], wo_ref[cc:, :])
    o_ref[...] = h_ref[...] + m * _rms_scale(m) * gpost_ref[...]


def _outproj(h, mix_c, mix_r, wo, g_post):
    m, d = h.shape
    tm = _row_tile(m, TILES["outproj_tm"])

    def rows(width):
        return pl.BlockSpec((tm, width), lambda i: (i, 0))

    return pl.pallas_call(
        _outproj_body,
        out_shape=jax.ShapeDtypeStruct((m, d), F32),
        grid=(m // tm,),
        in_specs=[rows(d), rows(mix_c.shape[1]), rows(mix_r.shape[1]),
                  pl.BlockSpec(wo.shape, lambda i: (0, 0), pipeline_mode=pl.Buffered(1)),
                  pl.BlockSpec(g_post.shape, lambda i: (0, 0))],
        out_specs=rows(d),
        compiler_params=_cparams(("parallel",)),
        name="outproj",
    )(h, mix_c, mix_r, wo, g_post)


def _prepare_weights(g_ffn1_pre, w_ffn1_gate, w_ffn1_up, w_ffn1_down, g_ffn1_post, g_mix_pre, w_in, conv_w,
                     tshift_mu, w0, w_decay_up, a0, w_a_up, w_g_up, k_k, k_a, r_k, ln_x_w, ln_x_b, w_out,
                     g_mix_post, g_ffn2_pre, w_ffn2_gate, w_ffn2_up, w_ffn2_down, g_ffn2_post):
    rd = w0.shape[-1]
    dl, al, gl = w_decay_up.shape[0], w_a_up.shape[0], w_g_up.shape[0]
    assert dl + al == LANES and gl <= LORA_BLOCK - LANES and rd % HEAD_DIM == 0
    lora = dl + al + gl
    gl_pad = _round_up(gl, LANES)

    def row(vec):
        return vec.reshape(1, -1).astype(F32)

    def ffn_w(wg, wu, wd):
        return wg.astype(BF16), wu.astype(BF16), wd.astype(BF16)

    head = jnp.arange(MXU_DIM, dtype=jnp.int32) // HEAD_DIM
    return dict(
        ffn1=(row(g_ffn1_pre),) + ffn_w(w_ffn1_gate, w_ffn1_up, w_ffn1_down) + (row(g_ffn1_post),),
        ffn2_gains=(row(g_ffn2_pre), row(g_ffn2_post)),
        ffn2_f32=(w_ffn2_gate, w_ffn2_up, w_ffn2_down),
        g_mix_pre=row(g_mix_pre),
        w_in=jnp.pad(w_in.astype(BF16), ((0, 0), (0, LORA_BLOCK - lora))),
        lora=lora,
        conv_w=conv_w.astype(F32),
        mu=jnp.pad(row(tshift_mu), ((0, 0), (0, LORA_BLOCK - lora))),
        w0=row(w0),
        wdec=jnp.pad(w_decay_up.astype(BF16), ((0, al), (0, 0))),
        a0=row(a0),
        wa=jnp.pad(w_a_up.astype(BF16), ((dl, 0), (0, 0))),
        wg=jnp.pad(w_g_up.astype(BF16), ((0, gl_pad - gl), (0, 0))),
        k_k=row(k_k), k_a=row(k_a), r_k=row(r_k),
        bd=(head[:, None] == head[None, :]).astype(BF16),
        ln_w=row(ln_x_w), ln_b=row(ln_x_b),
        w_out=w_out.astype(BF16),
        g_mix_post=row(g_mix_post),
    )


def _layer(x3, conv_prev, shift_prev, wkv_prev, wts, ffn2_w=None):
    bsz, t, d = x3.shape
    m = bsz * t
    lora = wts["lora"]
    x = x3.reshape(m, d)
    h = _ffn(x, *wts["ffn1"])
    p = _inproj(h, wts["g_mix_pre"], wts["w_in"])
    shift_pad = jnp.pad(shift_prev.astype(F32), ((0, 0), (0, 0), (0, LORA_BLOCK - lora)))
    side = () if ffn2_w is not None else wts["ffn2_f32"]
    mix_c, mix_r, new_conv, new_shift, new_wkv, *cast = _mixer(
        p.reshape(bsz, t, -1), conv_prev.astype(F32), shift_pad, wkv_prev.astype(F32), wts, side)
    if ffn2_w is None:
        ffn2_w = tuple(cast)
    h = _outproj(h, mix_c.reshape(m, -1), mix_r.reshape(m, -1), wts["w_out"], wts["g_mix_post"])
    g_pre, g_post = wts["ffn2_gains"]
    out = _ffn(h, g_pre, *ffn2_w, g_post)
    shift_dim = shift_prev.shape[-1]
    return (out.reshape(bsz, t, d), new_conv, new_shift[:, :, :shift_dim], new_wkv), ffn2_w


def kernel(x_prompt, x_sample, cache_conv, cache_shift, state_wkv, g_ffn1_pre, w_ffn1_gate, w_ffn1_up, w_ffn1_down, g_ffn1_post, g_mix_pre, w_in, conv_w, tshift_mu, w0, w_decay_up, a0, w_a_up, w_g_up, k_k, k_a, r_k, ln_x_w, ln_x_b, w_out, g_mix_post, g_ffn2_pre, w_ffn2_gate, w_ffn2_up, w_ffn2_down, g_ffn2_post):
    depth = w_in.shape[0]
    bp = x_prompt.shape[0]
    conv_rows, conv_ch = cache_conv.shape[2], cache_conv.shape[3]
    shift_dim = cache_shift.shape[-1]
    nh, hd = state_wkv.shape[2], state_wkv.shape[3]
    per_layer = (g_ffn1_pre, w_ffn1_gate, w_ffn1_up, w_ffn1_down, g_ffn1_post, g_mix_pre, w_in, conv_w,
                 tshift_mu, w0, w_decay_up, a0, w_a_up, w_g_up, k_k, k_a, r_k, ln_x_w, ln_x_b, w_out,
                 g_mix_post, g_ffn2_pre, w_ffn2_gate, w_ffn2_up, w_ffn2_down, g_ffn2_post)
    yp, ys = x_prompt, x_sample
    outs = [[] for _ in range(6)]
    for l in range(depth):
        wts = _prepare_weights(*(w[l] for w in per_layer))
        zc = jnp.zeros((bp, conv_rows, conv_ch), F32)
        zs = jnp.zeros((bp, 1, shift_dim), F32)
        zw = jnp.zeros((bp, nh, hd, hd), F32)
        (yp, c1, s1, w1), ffn2_w = _layer(yp, zc, zs, zw, wts)
        (ys, c2, s2, w2), _ = _layer(ys, cache_conv[l], cache_shift[l], state_wkv[l], wts, ffn2_w)
        for lst, val, ref in zip(outs, (c1, s1, w1, c2, s2, w2),
                                 (cache_conv, cache_shift, state_wkv, cache_conv, cache_shift, state_wkv)):
            lst.append(val.astype(ref.dtype))
    return (yp, ys) + tuple(jnp.stack(o, 0) for o in outs)
```

```python
import functools

import jax
import jax.numpy as jnp
from jax import lax
from jax.experimental import pallas as pl
from jax.experimental.pallas import tpu as pltpu

F32 = jnp.float32
BF16 = jnp.bfloat16

RMS_EPS = 1e-6
GN_EPS = 64e-5
L2_EPS = 1e-12
HEAD_DIM = 64
WKV_CHUNK = 64
MXU_DIM = 256
WKV_GROUP = MXU_DIM // HEAD_DIM
PIPE_CHUNKS = 2
LANES = 128
LORA_BLOCK = 512
VMEM_LIMIT = 56 * 1024 * 1024
TILES = dict(ffn_tm=512, ffn_tf=512, inproj_tm=1024, inproj_tn=1664, mixer_tt=256, outproj_tm=512)
RESIDENT_ROWS = 1024


def _cparams(sem):
    return pltpu.CompilerParams(dimension_semantics=sem, vmem_limit_bytes=VMEM_LIMIT)


def _round_up(n, m):
    return (n + m - 1) // m * m


def _row_tile(m, pref):
    return pref if m % pref == 0 else m


def _rows_plan(m, pref):
    if m <= RESIDENT_ROWS or m % pref:
        return m, pl.Buffered(1)
    return pref, None


def _rms_scale(x):
    return lax.rsqrt(jnp.mean(x * x, axis=-1, keepdims=True) + RMS_EPS)


def _dot(a, b):
    return jnp.dot(a, b, preferred_element_type=F32)


def _dot_nt(a, b):
    return lax.dot_general(a, b, (((1,), (1,)), ((), ())), preferred_element_type=F32)


def _dot_tn(a, b):
    return lax.dot_general(a, b, (((0,), (0,)), ((), ())), preferred_element_type=F32)


def _split2(x):
    hi = x.astype(BF16)
    lo = (x - hi.astype(F32)).astype(BF16)
    return hi, lo


def _head_sum(x, bd):
    w = bd.shape[0]
    rows = x.shape[0]
    starts = range(0, x.shape[1], w)
    stacked = jnp.concatenate([x[:, s:s + w] for s in starts], axis=0).astype(BF16)
    sums = _dot(stacked, bd)
    return jnp.concatenate([sums[i * rows:(i + 1) * rows] for i in range(len(starts))], axis=1)


def _ffn_body(x_ref, gpre_ref, wg_ref, wu_ref, wd_ref, *rest, nb_main, has_tail):
    if has_tail:
        wgt_ref, wut_ref, wdt_ref, gpost_ref, o_ref, xn_ref, acc_ref = rest
    else:
        gpost_ref, o_ref, xn_ref, acc_ref = rest
    j = pl.program_id(1)

    @pl.when(j == 0)
    def _init():
        x = x_ref[...]
        xn_ref[...] = (x * _rms_scale(x) * gpre_ref[...]).astype(BF16)
        acc_ref[...] = jnp.zeros_like(acc_ref)

    def accumulate(wg, wu, wd):
        xn = xn_ref[...]
        gate = _dot(xn, wg[...])
        up = _dot(xn, wu[...])
        hmid = (gate * jax.nn.sigmoid(gate) * up).astype(BF16)
        acc_ref[...] += _dot(hmid, wd[...])

    if has_tail:
        pl.when(j < nb_main)(lambda: accumulate(wg_ref, wu_ref, wd_ref))
        pl.when(j == nb_main)(lambda: accumulate(wgt_ref, wut_ref, wdt_ref))
    else:
        accumulate(wg_ref, wu_ref, wd_ref)

    @pl.when(j == pl.num_programs(1) - 1)
    def _fin():
        a = acc_ref[...]
        o_ref[...] = x_ref[...] + 0.5 * (a * _rms_scale(a) * gpost_ref[...])


def _ffn(x, g_pre, wg, wu, wd, g_post):
    m, d = x.shape
    f = wg.shape[1]
    tm, rows_mode = _rows_plan(m, TILES["ffn_tm"])
    tf = min(TILES["ffn_tf"], f)
    nb_main = f // tf
    f_main = nb_main * tf
    assert (f - f_main) % LANES == 0
    has_tail = f_main < f
    last_main = nb_main - 1

    def whole(arr):
        return pl.BlockSpec(arr.shape, lambda i, j: (0, 0), pipeline_mode=pl.Buffered(1))

    tails = [wg[:, f_main:], wu[:, f_main:], wd[f_main:, :]] if has_tail else []
    return pl.pallas_call(
        functools.partial(_ffn_body, nb_main=nb_main, has_tail=has_tail),
        out_shape=jax.ShapeDtypeStruct((m, d), F32),
        grid=(m // tm, nb_main + has_tail),
        in_specs=[
            pl.BlockSpec((tm, d), lambda i, j: (i, 0), pipeline_mode=rows_mode),
            pl.BlockSpec((1, d), lambda i, j: (0, 0)),
            pl.BlockSpec((d, tf), lambda i, j: (0, jnp.minimum(j, last_main))),
            pl.BlockSpec((d, tf), lambda i, j: (0, jnp.minimum(j, last_main))),
            pl.BlockSpec((tf, d), lambda i, j: (jnp.minimum(j, last_main), 0)),
        ] + [whole(t) for t in tails] + [pl.BlockSpec((1, d), lambda i, j: (0, 0))],
        out_specs=pl.BlockSpec((tm, d), lambda i, j: (i, 0), pipeline_mode=rows_mode),
        scratch_shapes=[pltpu.VMEM((tm, d), BF16), pltpu.VMEM((tm, d), F32)],
        compiler_params=_cparams(("parallel", "arbitrary")),
        name="ffn",
    )(x, g_pre, wg, wu, wd, *tails, g_post)


def _inproj_body(x_ref, g_ref, w_ref, o_ref, xn_ref):
    @pl.when(pl.program_id(1) == 0)
    def _init():
        x = x_ref[...]
        xn_ref[...] = (x * _rms_scale(x) * g_ref[...]).astype(BF16)

    o_ref[...] = _dot(xn_ref[...], w_ref[...])


def _inproj(x, g, w):
    m, d = x.shape
    n = w.shape[1]
    tm, rows_mode = _rows_plan(m, TILES["inproj_tm"])
    tn = _row_tile(n, TILES["inproj_tn"])
    return pl.pallas_call(
        _inproj_body,
        out_shape=jax.ShapeDtypeStruct((m, n), F32),
        grid=(m // tm, n // tn),
        in_specs=[
            pl.BlockSpec((tm, d), lambda i, j: (i, 0), pipeline_mode=rows_mode),
            pl.BlockSpec((1, d), lambda i, j: (0, 0)),
            pl.BlockSpec((d, tn), lambda i, j: (0, j)),
        ],
        out_specs=pl.BlockSpec((tm, tn), lambda i, j: (i, j)),
        scratch_shapes=[pltpu.VMEM((tm, d), BF16)],
        compiler_params=_cparams(("parallel", "arbitrary")),
        name="inproj",
    )(x, g, w)


def _shift_rows(x, carry_rows):
    n = len(carry_rows)
    out = pltpu.roll(x, n, axis=0)
    row = lax.broadcasted_iota(jnp.int32, x.shape, 0)
    for i, c in enumerate(carry_rows):
        out = jnp.where(row == i, c, out)
    return out


def _bd(y, mask):
    reps = mask.shape[0] // y.shape[0]
    return jnp.where(mask, jnp.concatenate([y] * reps, axis=0), jnp.zeros((), y.dtype))


def _mm_heads(x, y, mask, nt=False):
    dot = _dot_nt if nt else _dot
    return dot(x.astype(BF16), _bd(y.astype(BF16), mask))


def _cast_block(src_ref, dst_ref):
    dst_ref[...] = src_ref[...].astype(dst_ref.dtype)


def _side_blocks(rows, steps):
    return max(n for n in range(1, steps + 1) if rows % n == 0 and (rows // n) % 16 == 0)


def _round_robin(tasks):
    while tasks:
        tasks = [t for t in tasks if next(t, tasks) is not tasks]


def _mixer_body(*refs, nside, side_blocks):
    n_in = 23
    (pb_ref, pc_ref, ph_ref, pr_ref, pk_ref, pv_ref, pl_ref, cprev_ref, sprev_ref, s0_ref,
     cw_ref, mu_ref, w0_ref, wdec_ref, a0_ref, wa_ref, wg_ref, kk_ref, ka_ref, rk_ref,
     lnw_ref, lnb_ref, bd_ref) = refs[:n_in]
    side_in = refs[n_in:n_in + nside]
    mixc_ref, mixr_ref, nconv_ref, nshift_ref, sout_ref = refs[n_in + nside:n_in + nside + 5]
    side_out = refs[n_in + nside + 5:n_in + 2 * nside + 5]
    (cu_scr, cs_scr, s_scr, r_scr, lw_scr, k_scr, v_scr, kn_scr, b_scr, y_scr, bonus_scr,
     g_scr) = refs[n_in + 2 * nside + 5:]
    ti = pl.program_id(1)
    step_no = pl.program_id(0) * pl.num_programs(1) + ti
    for src, dst, nblk in zip(side_in, side_out, side_blocks):
        pl.when(step_no < nblk)(functools.partial(_cast_block, src, dst))
    c = WKV_CHUNK
    n = HEAD_DIM
    rd = pr_ref.shape[-1]
    nchunk = pc_ref.shape[1] // c
    wdt = WKV_GROUP * n
    ngroups = rd // wdt

    @pl.when(ti == 0)
    def _init():
        cu_scr[...] = cprev_ref[0]
        cs_scr[...] = sprev_ref[0]
        for h in range(rd // n):
            s_scr[:, h * n:(h + 1) * n] = s0_ref[0, h]

    bd = bd_ref[...]
    row = lax.broadcasted_iota(jnp.int32, (c, c), 0)
    col = lax.broadcasted_iota(jnp.int32, (c, c), 1)
    ltri = jnp.where(row >= col, 1.0, 0.0).astype(BF16)
    prow = lax.broadcasted_iota(jnp.int32, (c, wdt), 0)
    pcol = lax.broadcasted_iota(jnp.int32, (c, wdt), 1) % n
    strict = prow > pcol
    incl = prow >= pcol
    brow = lax.broadcasted_iota(jnp.int32, (wdt, wdt), 0) // n
    bcol = lax.broadcasted_iota(jnp.int32, (wdt, wdt), 1) // n
    same_head = brow == bcol

    def prep_chunk(sl):
        u = pc_ref[0, sl, :] * ph_ref[0, sl, :]
        c0 = cu_scr[0:1, :]
        c1 = cu_scr[1:2, :]
        u2 = _shift_rows(u, [c0, c1])
        u1 = _shift_rows(u, [c1])
        conv = u2 * cw_ref[0:1, :] + u1 * cw_ref[1:2, :] + u * cw_ref[2:3, :]
        mixc_ref[0, sl, :] = (pb_ref[0, sl, :] * conv).astype(mixc_ref.dtype)
        nr = u.shape[0]
        cu_scr[...] = u[nr - 2:nr, :]
        yield

        def shifted(p, lo, hi):
            prev = _shift_rows(p, [cs_scr[0:1, lo:hi]])
            q = p + (prev - p) * mu_ref[0:1, lo:hi]
            cs_scr[0:1, lo:hi] = p[nr - 1:nr, :]
            return q

        ql = shifted(pl_ref[0, sl, :], 3 * rd, 3 * rd + LORA_BLOCK)
        ql_da = ql[:, 0:LANES]
        dec_in = w0_ref[...] + _dot(jnp.tanh(ql_da).astype(BF16), wdec_ref[...])
        a = jax.nn.sigmoid(a0_ref[...] + _dot(ql_da.astype(BF16), wa_ref[...]))
        g_scr[sl, :] = _dot(jax.nn.sigmoid(ql[:, LANES:LANES + wg_ref.shape[0]]).astype(BF16), wg_ref[...])
        yield
        z = -dec_in
        softplus = jnp.maximum(z, 0.0) + jnp.log(1.0 + jnp.exp(-jnp.abs(z)))
        w_log = -softplus - 0.5
        lw_scr[sl, :] = -jnp.exp(w_log)
        k = shifted(pk_ref[0, sl, :], rd, 2 * rd)
        kk = k * kk_ref[...]
        ssq = _head_sum(kk * kk, bd)
        yield
        kk = kk / jnp.maximum(jnp.sqrt(ssq), L2_EPS)
        kn_scr[sl, :] = kk
        b_scr[sl, :] = kk * a
        k = k * (1.0 + (a - 1.0) * ka_ref[...])
        k_scr[sl, :] = k
        r = shifted(pr_ref[0, sl, :], 0, rd)
        r_scr[sl, :] = r
        rk_sum = _head_sum(r * k * rk_ref[...], bd)
        yield
        v = shifted(pv_ref[0, sl, :], 2 * rd, 3 * rd)
        v_scr[sl, :] = v
        bonus_scr[sl, :] = rk_sum * v

    def wkv_pre(sl, ln, out):
        lw = lw_scr[sl, ln]
        l_hi, l_lo = _split2(lw)
        cum = _dot(ltri, jnp.concatenate([l_hi, l_lo], axis=1))
        cum = cum[:, :wdt] + cum[:, wdt:]
        yield
        cum_last = cum[c - 1:c, :]
        g_end = jnp.exp(cum_last - cum)
        g_inv = jnp.exp(-cum)
        kk = k_scr[sl, ln]
        vv = v_scr[sl, ln]
        bb = b_scr[sl, ln]
        ap = jnp.concatenate([kn_scr[sl, ln] * jnp.exp(cum - lw), r_scr[sl, ln] * jnp.exp(cum)], axis=0)
        g_b = _mm_heads(ap, bb * g_inv, same_head, nt=True)
        g_k = _mm_heads(ap, kk * g_inv, same_head, nt=True)
        yield
        q = jnp.where(strict, -g_b[:c], 0.0)
        lk = jnp.concatenate([jnp.where(strict, g_k[:c], 0.0), jnp.where(incl, g_k[c:], 0.0)], axis=0)
        t1v = _mm_heads(lk, vv, same_head)
        z = q
        q = _mm_heads(q, q, same_head)
        yield
        for level in range(5):
            if level < 4:
                zq_qq = _mm_heads(jnp.concatenate([z, q], axis=0), q, same_head)
                z, q = z + q + zq_qq[:c], zq_qq[c:]
            else:
                z = z + q + _mm_heads(z, q, same_head)
            yield
        out.update(ap=ap.astype(BF16), t1v=t1v, z=z.astype(BF16), irb=jnp.where(incl, g_b[c:], 0.0).astype(BF16),
                   vv=vv, kb=jnp.concatenate([kk * g_end, bb * g_end], axis=0).astype(BF16),
                   g_last=jnp.exp(cum_last))

    def wkv_post(sl, ln, pre):
        s0 = s_scr[:, ln]
        a_s = _dot_nt(pre["ap"], _bd(s0.astype(BF16), same_head))
        yield
        t1 = a_s + pre["t1v"]
        u = t1[:c] + _dot(pre["z"], _bd(t1[:c].astype(BF16), same_head))
        yield
        ub = u.astype(BF16)
        y_scr[sl, ln] = t1[c:] - _dot(pre["irb"], _bd(ub, same_head))
        vu = jnp.concatenate([pre["vv"].astype(BF16), -ub], axis=0)
        gram = jnp.where(same_head, _dot_tn(vu, pre["kb"]), 0.0)
        upd = gram[0:n]
        for h in range(1, WKV_GROUP):
            upd = upd + gram[h * n:(h + 1) * n]
        s_scr[:, ln] = s0 * pre["g_last"] + upd

    def out_chunk(sl):
        y = y_scr[sl, :]
        inv_n = 1.0 / HEAD_DIM
        mean = _head_sum(y, bd) * inv_n
        yield
        yc = y - mean
        var = _head_sum(yc * yc, bd) * inv_n
        yield
        yn = yc * lax.rsqrt(var + GN_EPS) * lnw_ref[...] + lnb_ref[...]
        mixr_ref[0, sl, :] = ((yn + bonus_scr[sl, :]) * g_scr[sl, :]).astype(mixr_ref.dtype)

    def rows(i):
        return slice(i * c, (i + 1) * c)

    def in_turn(gens):
        for gen in gens:
            yield from gen

    lanes = [slice(g * wdt, (g + 1) * wdt) for g in range(ngroups)]
    pre = [[{} for _ in lanes] for _ in range(nchunk)]
    span = PIPE_CHUNKS if nchunk % PIPE_CHUNKS == 0 else 1
    nstage = nchunk // span

    def chunks_of(stage):
        return range(stage * span, (stage + 1) * span) if 0 <= stage < nstage else ()

    for step in range(nstage + 3):
        tasks = [in_turn([wkv_post(rows(i), ln, pre[i][g]) for i in chunks_of(step - 2)])
                 for g, ln in enumerate(lanes)]
        tasks += [wkv_pre(rows(i), ln, pre[i][g]) for i in chunks_of(step - 1) for g, ln in enumerate(lanes)]
        tasks.append(in_turn([prep_chunk(rows(i)) for i in chunks_of(step)]))
        tasks.append(in_turn([out_chunk(rows(i)) for i in chunks_of(step - 3)]))
        _round_robin(tasks)

    @pl.when(ti == pl.num_programs(1) - 1)
    def _fin():
        nconv_ref[0] = cu_scr[...]
        nshift_ref[0] = cs_scr[...]
        for h in range(rd // n):
            sout_ref[0, h] = s_scr[:, h * n:(h + 1) * n]


def _mixer(p3, conv_prev, shift_prev, s0, wts, side=()):
    bsz, t, _ = p3.shape
    cc = conv_prev.shape[-1]
    rd = wts["w0"].shape[-1]
    sp = shift_prev.shape[-1]
    tt = _row_tile(t, TILES["mixer_tt"])
    assert tt % WKV_CHUNK == 0 and rd % (WKV_GROUP * HEAD_DIM) == 0
    lora_blk = (3 * cc + 3 * rd) // LORA_BLOCK
    nt = t // tt

    def col(width, idx):
        return pl.BlockSpec((1, tt, width), lambda b, i: (b, i, idx))

    def const(arr):
        return pl.BlockSpec(arr.shape, lambda b, i: (0,) * arr.ndim)

    def per_stream(rows, width):
        return pl.BlockSpec((1, rows, width), lambda b, i: (b, 0, 0))

    state_spec = pl.BlockSpec((1,) + s0.shape[1:], lambda b, i: (b, 0, 0, 0))
    side_blocks = tuple(_side_blocks(w.shape[0], bsz * nt) for w in side)

    def side_spec(w, nblk):
        return pl.BlockSpec((w.shape[0] // nblk, w.shape[1]), lambda b, i: (jnp.minimum(b * nt + i, nblk - 1), 0))

    side_specs = [side_spec(w, nblk) for w, nblk in zip(side, side_blocks)]
    names = ("conv_w", "mu", "w0", "wdec", "a0", "wa", "wg", "k_k", "k_a", "r_k", "ln_w", "ln_b", "bd")
    consts = [wts[n] for n in names]
    tok = pltpu.VMEM((tt, rd), F32)
    return pl.pallas_call(
        functools.partial(_mixer_body, nside=len(side), side_blocks=side_blocks),
        out_shape=[jax.ShapeDtypeStruct((bsz, t, cc), BF16), jax.ShapeDtypeStruct((bsz, t, rd), BF16),
                   jax.ShapeDtypeStruct((bsz, 2, cc), F32), jax.ShapeDtypeStruct((bsz, 1, sp), F32),
                   jax.ShapeDtypeStruct(s0.shape, F32)]
        + [jax.ShapeDtypeStruct(w.shape, BF16) for w in side],
        grid=(bsz, nt),
        in_specs=[col(cc, 0), col(cc, 1), col(cc, 2), col(rd, 3), col(rd, 4), col(rd, 5),
                  col(LORA_BLOCK, lora_blk),
                  per_stream(2, cc), per_stream(1, sp), state_spec]
        + [const(c) for c in consts] + side_specs,
        out_specs=[pl.BlockSpec((1, tt, cc), lambda b, i: (b, i, 0)), pl.BlockSpec((1, tt, rd), lambda b, i: (b, i, 0)),
                   per_stream(2, cc), per_stream(1, sp), state_spec] + side_specs,
        scratch_shapes=[pltpu.VMEM((2, cc), F32), pltpu.VMEM((1, sp), F32), pltpu.VMEM((HEAD_DIM, rd), F32)]
        + [tok] * 9,
        compiler_params=_cparams(("arbitrary", "arbitrary")),
        name="mixer",
    )(p3, p3, p3, p3, p3, p3, p3, conv_prev, shift_prev, s0, *consts, *side)


def _outproj_body(h_ref, mc_ref, mr_ref, wo_ref, gpost_ref, o_ref):
    cc = mc_ref.shape[-1]
    m = _dot(mc_ref[...], wo_ref[0:cc, :]) + _dot(mr_ref[...], wo_ref[cc:, :])
    o_ref[...] = h_ref[...] + m * _rms_scale(m) * gpost_ref[...]


def _outproj(h, mix_c, mix_r, wo, g_post):
    m, d = h.shape
    tm = _row_tile(m, TILES["outproj_tm"])

    def rows(width):
        return pl.BlockSpec((tm, width), lambda i: (i, 0))

    return pl.pallas_call(
        _outproj_body,
        out_shape=jax.ShapeDtypeStruct((m, d), F32),
        grid=(m // tm,),
        in_specs=[rows(d), rows(mix_c.shape[1]), rows(mix_r.shape[1]),
                  pl.BlockSpec(wo.shape, lambda i: (0, 0), pipeline_mode=pl.Buffered(1)),
                  pl.BlockSpec(g_post.shape, lambda i: (0, 0))],
        out_specs=rows(d),
        compiler_params=_cparams(("parallel",)),
        name="outproj",
    )(h, mix_c, mix_r, wo, g_post)


def _prepare_weights(g_ffn1_pre, w_ffn1_gate, w_ffn1_up, w_ffn1_down, g_ffn1_post, g_mix_pre, w_in, conv_w,
                     tshift_mu, w0, w_decay_up, a0, w_a_up, w_g_up, k_k, k_a, r_k, ln_x_w, ln_x_b, w_out,
                     g_mix_post, g_ffn2_pre, w_ffn2_gate, w_ffn2_up, w_ffn2_down, g_ffn2_post):
    rd = w0.shape[-1]
    dl, al, gl = w_decay_up.shape[0], w_a_up.shape[0], w_g_up.shape[0]
    assert dl + al == LANES and gl <= LORA_BLOCK - LANES and rd % HEAD_DIM == 0
    lora = dl + al + gl
    gl_pad = _round_up(gl, LANES)

    def row(vec):
        return vec.reshape(1, -1).astype(F32)

    def ffn_w(wg, wu, wd):
        return wg.astype(BF16), wu.astype(BF16), wd.astype(BF16)

    head = jnp.arange(MXU_DIM, dtype=jnp.int32) // HEAD_DIM
    return dict(
        ffn1=(row(g_ffn1_pre),) + ffn_w(w_ffn1_gate, w_ffn1_up, w_ffn1_down) + (row(g_ffn1_post),),
        ffn2_gains=(row(g_ffn2_pre), row(g_ffn2_post)),
        late_f32=(w_ffn2_gate, w_ffn2_up, w_ffn2_down, w_out),
        g_mix_pre=row(g_mix_pre),
        w_in=jnp.pad(w_in.astype(BF16), ((0, 0), (0, LORA_BLOCK - lora))),
        lora=lora,
        conv_w=conv_w.astype(F32),
        mu=jnp.pad(row(tshift_mu), ((0, 0), (0, LORA_BLOCK - lora))),
        w0=row(w0),
        wdec=jnp.pad(w_decay_up.astype(BF16), ((0, al), (0, 0))),
        a0=row(a0),
        wa=jnp.pad(w_a_up.astype(BF16), ((dl, 0), (0, 0))),
        wg=jnp.pad(w_g_up.astype(BF16), ((0, gl_pad - gl), (0, 0))),
        k_k=row(k_k), k_a=row(k_a), r_k=row(r_k),
        bd=(head[:, None] == head[None, :]).astype(BF16),
        ln_w=row(ln_x_w), ln_b=row(ln_x_b),
        g_mix_post=row(g_mix_post),
    )


def _layer(x3, conv_prev, shift_prev, wkv_prev, wts, late_w=None):
    bsz, t, d = x3.shape
    m = bsz * t
    lora = wts["lora"]
    x = x3.reshape(m, d)
    h = _ffn(x, *wts["ffn1"])
    p = _inproj(h, wts["g_mix_pre"], wts["w_in"])
    shift_pad = jnp.pad(shift_prev.astype(F32), ((0, 0), (0, 0), (0, LORA_BLOCK - lora)))
    side = () if late_w is not None else wts["late_f32"]
    mix_c, mix_r, new_conv, new_shift, new_wkv, *cast = _mixer(
        p.reshape(bsz, t, -1), conv_prev.astype(F32), shift_pad, wkv_prev.astype(F32), wts, side)
    if late_w is None:
        late_w = tuple(cast)
    *ffn2_w, w_out = late_w
    h = _outproj(h, mix_c.reshape(m, -1), mix_r.reshape(m, -1), w_out, wts["g_mix_post"])
    g_pre, g_post = wts["ffn2_gains"]
    out = _ffn(h, g_pre, *ffn2_w, g_post)
    shift_dim = shift_prev.shape[-1]
    return (out.reshape(bsz, t, d), new_conv, new_shift[:, :, :shift_dim], new_wkv), late_w


def kernel(x_prompt, x_sample, cache_conv, cache_shift, state_wkv, g_ffn1_pre, w_ffn1_gate, w_ffn1_up, w_ffn1_down, g_ffn1_post, g_mix_pre, w_in, conv_w, tshift_mu, w0, w_decay_up, a0, w_a_up, w_g_up, k_k, k_a, r_k, ln_x_w, ln_x_b, w_out, g_mix_post, g_ffn2_pre, w_ffn2_gate, w_ffn2_up, w_ffn2_down, g_ffn2_post):
    depth = w_in.shape[0]
    bp = x_prompt.shape[0]
    conv_rows, conv_ch = cache_conv.shape[2], cache_conv.shape[3]
    shift_dim = cache_shift.shape[-1]
    nh, hd = state_wkv.shape[2], state_wkv.shape[3]
    per_layer = (g_ffn1_pre, w_ffn1_gate, w_ffn1_up, w_ffn1_down, g_ffn1_post, g_mix_pre, w_in, conv_w,
                 tshift_mu, w0, w_decay_up, a0, w_a_up, w_g_up, k_k, k_a, r_k, ln_x_w, ln_x_b, w_out,
                 g_mix_post, g_ffn2_pre, w_ffn2_gate, w_ffn2_up, w_ffn2_down, g_ffn2_post)
    yp, ys = x_prompt, x_sample
    outs = [[] for _ in range(6)]
    for l in range(depth):
        wts = _prepare_weights(*(w[l] for w in per_layer))
        zc = jnp.zeros((bp, conv_rows, conv_ch), F32)
        zs = jnp.zeros((bp, 1, shift_dim), F32)
        zw = jnp.zeros((bp, nh, hd, hd), F32)
        (yp, c1, s1, w1), late_w = _layer(yp, zc, zs, zw, wts)
        (ys, c2, s2, w2), _ = _layer(ys, cache_conv[l], cache_shift[l], state_wkv[l], wts, late_w)
        for lst, val, ref in zip(outs, (c1, s1, w1, c2, s2, w2),
                                 (cache_conv, cache_shift, state_wkv, cache_conv, cache_shift, state_wkv)):
            lst.append(val.astype(ref.dtype))
    return (yp, ys) + tuple(jnp.stack(o, 0) for o in outs)
```
